```python
import math
import jax, jax.numpy as jnp
from jax import lax
import numpy as np

D_MODEL = 1024
BATCH = 16
SEQ = 4096
DEPTH = 2

N_META = 16
CHUNK = 64
N_PAD = CHUNK - N_META
GDN_HEADS = D_MODEL // 128
GDN_DK = 128
GDN_DV = 128
GDN_QK = GDN_HEADS * GDN_DK
GDN_V = GDN_HEADS * GDN_DV
CONV_K = 5
CONV_DIM = 2 * GDN_QK + GDN_V
RET_HEADS = D_MODEL // 256
RET_DK = 256
RET_DV = 512
RET_QK = RET_HEADS * RET_DK
RET_V = RET_HEADS * RET_DV
ROPE_BASE = 10000.0
FFN_HIDDEN = -(-8 * D_MODEL // (3 * 256)) * 256
SPLIT_SIZES = (CONV_DIM, GDN_V, 2 * GDN_HEADS, 2 * GDN_HEADS, RET_QK, RET_QK, RET_V, RET_V, D_MODEL, D_MODEL)
N_IN = CONV_DIM + GDN_V + 4 * GDN_HEADS + 2 * RET_QK + 2 * RET_V + 2 * D_MODEL
EPS = 1e-6

kernel_name = "hybrid_gdn_retention_encoder"


def rms(x):
    xf = x.astype(jnp.float32)
    return xf * lax.rsqrt(jnp.mean(xf * xf, axis=-1, keepdims=True) + EPS)


def rms_norm(x, gain):
    return (rms(x) * gain.astype(jnp.float32)).astype(x.dtype)


def l2norm(x):
    return x * lax.rsqrt(jnp.sum(x * x, axis=-1, keepdims=True) + EPS)


def split_points(sizes):
    return [int(v) for v in np.cumsum(np.array(sizes))[:-1]]


def short_conv(x, w):
    c = x.shape[-1]
    pad = (w.shape[0] - 1) // 2
    y = lax.conv_general_dilated(x, w[:, None, :].astype(x.dtype), window_strides=(1,),
                                 padding=[(pad, pad)], dimension_numbers=('NWC', 'WIO', 'NWC'),
                                 feature_group_count=c)
    return jax.nn.silu(y)


def rope(t, pos):
    half = t.shape[-1] // 2
    inv = ROPE_BASE ** (-jnp.arange(half, dtype=jnp.float32) / half)
    ang = pos.astype(jnp.float32)[:, None] * inv[None, :]
    cos = jnp.cos(ang)[None, :, None, :]
    sin = jnp.sin(ang)[None, :, None, :]
    t1, t2 = t[..., :half], t[..., half:]
    return jnp.concatenate([t1 * cos - t2 * sin, t1 * sin + t2 * cos], axis=-1)


def pad_front(t):
    pads = [(0, 0)] * t.ndim
    pads[1] = (N_PAD, 0)
    return jnp.pad(t, pads)


def flip(t):
    return jnp.flip(t, axis=1)


def to_chunks(t):
    b, lp, h, d = t.shape
    return t.reshape(b, lp // CHUNK, CHUNK, h, d).transpose(1, 0, 3, 2, 4)


def from_chunks(t):
    n, b, h, c, d = t.shape
    return t.transpose(1, 0, 3, 2, 4).reshape(b, n * c, h, d)


def gated_delta_chunked(q, k, v, g, beta):
    dk = q.shape[-1]
    b, _, h, dv = v.shape
    q = to_chunks(q * (dk ** -0.5))
    k = to_chunks(k)
    v = to_chunks(v)
    g = to_chunks(g[..., None])[..., 0]
    beta = to_chunks(beta[..., None])[..., 0]
    gc = jnp.cumsum(g, axis=-1)
    idx = jnp.arange(CHUNK)
    lower_incl = idx[:, None] >= idx[None, :]
    lower_strict = idx[:, None] > idx[None, :]
    diff = gc[..., :, None] - gc[..., None, :]
    decay = jnp.where(lower_incl, jnp.exp(jnp.where(lower_incl, diff, 0.0)), 0.0)
    kb = k * beta[..., None]
    m = jnp.einsum('nbhid,nbhjd->nbhij', kb, k) * decay * lower_strict
    a = m + jnp.eye(CHUNK, dtype=m.dtype)
    u = lax.linalg.triangular_solve(a, v * beta[..., None], left_side=True, lower=True, unit_diagonal=True)
    w = lax.linalg.triangular_solve(a, kb * jnp.exp(gc)[..., None], left_side=True, lower=True, unit_diagonal=True)
    qk = jnp.einsum('nbhid,nbhjd->nbhij', q, k) * decay
    q_dec = q * jnp.exp(gc)[..., None]
    g_last = gc[..., -1]
    k_dec = k * jnp.exp(g_last[..., None] - gc)[..., None]

    def step(s, xs):
        u_i, w_i, qk_i, qd_i, kd_i, gl_i = xs
        v_new = u_i - jnp.einsum('bhcd,bhde->bhce', w_i, s)
        o = jnp.einsum('bhcd,bhde->bhce', qd_i, s) + jnp.einsum('bhij,bhje->bhie', qk_i, v_new)
        s = s * jnp.exp(gl_i)[..., None, None] + jnp.einsum('bhcd,bhce->bhde', kd_i, v_new)
        return s, o

    s0 = jnp.zeros((b, h, dk, dv), jnp.float32)
    _, o = lax.scan(step, s0, (u, w, qk, q_dec, k_dec, g_last))
    return from_chunks(o)


def retention_chunked(q, k, v, log_gamma):
    dk = q.shape[-1]
    b, _, h, dv = v.shape
    q = to_chunks(q)
    k = to_chunks(k * (dk ** -0.5))
    v = to_chunks(v)
    pos = jnp.arange(CHUNK, dtype=jnp.float32)
    lg = log_gamma.astype(jnp.float32)[:, None]
    lower_incl = pos[:, None] >= pos[None, :]
    rel = jnp.where(lower_incl, pos[:, None] - pos[None, :], 0.0)
    intra = jnp.where(lower_incl, jnp.exp(rel[None] * lg[..., None]), 0.0)
    qk = jnp.einsum('nbhid,nbhjd->nbhij', q, k) * intra
    q_dec = q * jnp.exp(lg * (pos + 1.0))[..., None]
    k_dec = k * jnp.exp(lg * (CHUNK - 1.0 - pos))[..., None]
    chunk_decay = jnp.exp(lg * CHUNK)[..., None]

    def step(r, xs):
        qd_i, kd_i, qk_i, v_i = xs
        o = jnp.einsum('bhcd,bhde->bhce', qd_i, r) + jnp.einsum('bhij,bhje->bhie', qk_i, v_i)
        r = r * chunk_decay + jnp.einsum('bhcd,bhce->bhde', kd_i, v_i)
        return r, o

    r0 = jnp.zeros((b, h, dk, dv), jnp.float32)
    _, o = lax.scan(step, r0, (q_dec, k_dec, qk, v))
    return from_chunks(o)


def token_mixer(h, w_in, conv_w, a_log, dt_bias, gdn_gain, ret_logit, w_up_a, w_up_b, w_out):
    b, l, _ = h.shape
    f32 = jnp.float32
    proj = h @ w_in.astype(h.dtype)
    (qkv_a, z_a, a_in, b_in, q_b, k_b, v_b, g_b, gate_a, gate_b) = jnp.split(proj, split_points(SPLIT_SIZES), axis=-1)

    qkv = short_conv(qkv_a, conv_w).astype(f32)
    qa, ka, va = jnp.split(qkv, [GDN_QK, 2 * GDN_QK], axis=-1)
    qa = l2norm(qa.reshape(b, l, GDN_HEADS, GDN_DK))
    ka = l2norm(ka.reshape(b, l, GDN_HEADS, GDN_DK))
    va = va.reshape(b, l, GDN_HEADS, GDN_DV)
    a_in = a_in.astype(f32).reshape(b, l, 2, GDN_HEADS)
    g = -jnp.exp(a_log.astype(f32)) * jax.nn.softplus(a_in + dt_bias.astype(f32))
    beta = jax.nn.sigmoid(b_in.astype(f32).reshape(b, l, 2, GDN_HEADS))
    qp, kp, vp, gp, bp = pad_front(qa), pad_front(ka), pad_front(va), pad_front(g), pad_front(beta)
    o_fwd = gated_delta_chunked(qp, kp, vp, gp[:, :, 0], bp[:, :, 0])
    o_bwd = flip(gated_delta_chunked(flip(qp), flip(kp), flip(vp), flip(gp[:, :, 1]), flip(bp[:, :, 1])))
    o_a = (o_fwd + o_bwd)[:, N_PAD:]
    o_a = rms(o_a) * gdn_gain.astype(f32) * jax.nn.silu(z_a.astype(f32).reshape(b, l, GDN_HEADS, GDN_DV))
    y_a = o_a.reshape(b, l, GDN_V).astype(h.dtype) @ w_up_a.astype(h.dtype)

    pos = jnp.arange(l)
    qr = rope(q_b.astype(f32).reshape(b, l, RET_HEADS, RET_DK), pos)
    kr = rope(k_b.astype(f32).reshape(b, l, RET_HEADS, RET_DK), pos)
    vr = v_b.astype(f32).reshape(b, l, RET_HEADS, RET_DV)
    log_gamma = jax.nn.log_sigmoid(ret_logit.astype(f32))
    qp, kp, vp = pad_front(qr), pad_front(kr), pad_front(vr)
    r_fwd = retention_chunked(qp, kp, vp, log_gamma[0])
    r_bwd = flip(retention_chunked(flip(qp), flip(kp), flip(vp), log_gamma[1]))
    o_b = rms((r_fwd + r_bwd)[:, N_PAD:])
    o_b = o_b.reshape(b, l, RET_V) * jax.nn.silu(g_b.astype(f32))
    y_b = o_b.astype(h.dtype) @ w_up_b.astype(h.dtype)

    merged = jax.nn.sigmoid(gate_a) * y_a + jax.nn.sigmoid(gate_b) * y_b
    return merged @ w_out.astype(h.dtype)


def swiglu(h, w_ffn_in, w_ffn_out):
    gate, up = jnp.split(h @ w_ffn_in.astype(h.dtype), 2, axis=-1)
    return (jax.nn.silu(gate) * up) @ w_ffn_out.astype(h.dtype)


def setup_inputs(seed: int = 0) -> dict:
    key = jax.random.key(seed)
    ks = jax.random.split(key, 20)
    f32 = jnp.float32

    def dense(k, shape, fan_in):
        return jax.random.normal(k, shape, f32) * (fan_in ** -0.5)

    def gain(k, shape):
        return 1.0 + 0.02 * jax.random.normal(k, shape, f32)

    x = jax.random.normal(ks[0], (BATCH, SEQ, D_MODEL), f32)
    meta_tokens = jax.random.normal(ks[1], (N_META, D_MODEL), f32)
    norm_mix = gain(ks[2], (DEPTH, D_MODEL))
    w_in = dense(ks[3], (DEPTH, D_MODEL, N_IN), D_MODEL)
    conv_w = dense(ks[4], (DEPTH, CONV_K, CONV_DIM), CONV_K)
    gdn_a_log = jnp.log(jax.random.uniform(ks[5], (DEPTH, 2, GDN_HEADS), f32, 1.0, 16.0))
    dt = jnp.exp(jax.random.uniform(ks[6], (DEPTH, 2, GDN_HEADS), f32, math.log(1e-3), math.log(1e-1)))
    gdn_dt_bias = dt + jnp.log(-jnp.expm1(-dt))
    gdn_norm = gain(ks[7], (DEPTH, GDN_DV))
    base_logit = jnp.log(2.0 ** (5.0 + jnp.arange(RET_HEADS, dtype=f32)) - 1.0)
    ret_decay_logit = base_logit + 0.1 * jax.random.normal(ks[8], (DEPTH, 2, RET_HEADS), f32)
    w_up_a = dense(ks[9], (DEPTH, GDN_V, D_MODEL), GDN_V)
    w_up_b = dense(ks[10], (DEPTH, RET_V, D_MODEL), RET_V)
    w_out = dense(ks[11], (DEPTH, D_MODEL, D_MODEL), D_MODEL)
    norm_ffn = gain(ks[12], (DEPTH, D_MODEL))
    w_ffn_in = dense(ks[13], (DEPTH, D_MODEL, 2 * FFN_HIDDEN), D_MODEL)
    w_ffn_out = dense(ks[14], (DEPTH, FFN_HIDDEN, D_MODEL), FFN_HIDDEN)
    norm_final = gain(ks[15], (D_MODEL,))
    return {"x": x, "meta_tokens": meta_tokens, "norm_mix": norm_mix, "w_in": w_in, "conv_w": conv_w,
            "gdn_a_log": gdn_a_log, "gdn_dt_bias": gdn_dt_bias, "gdn_norm": gdn_norm,
            "ret_decay_logit": ret_decay_logit, "w_up_a": w_up_a, "w_up_b": w_up_b, "w_out": w_out,
            "norm_ffn": norm_ffn, "w_ffn_in": w_ffn_in, "w_ffn_out": w_ffn_out, "norm_final": norm_final}


def reference(x, meta_tokens, norm_mix, w_in, conv_w, gdn_a_log, gdn_dt_bias, gdn_norm, ret_decay_logit,
              w_up_a, w_up_b, w_out, norm_ffn, w_ffn_in, w_ffn_out, norm_final):
    b = x.shape[0]
    meta = jnp.broadcast_to(meta_tokens.astype(x.dtype)[None], (b, N_META, x.shape[-1]))
    h = jnp.concatenate([meta, x], axis=1)
    for i in range(DEPTH):
        h = h + token_mixer(rms_norm(h, norm_mix[i]), w_in[i], conv_w[i], gdn_a_log[i], gdn_dt_bias[i],
                            gdn_norm[i], ret_decay_logit[i], w_up_a[i], w_up_b[i], w_out[i])
        h = h + swiglu(rms_norm(h, norm_ffn[i]), w_ffn_in[i], w_ffn_out[i])
    h = rms_norm(h, norm_final)
    return h[:, N_META:]
```

```python
import functools
import math

import jax
import jax.numpy as jnp
from jax import lax
from jax.experimental import pallas as pl
from jax.experimental.pallas import tpu as pltpu

N_META = 16
CONV_K = 5
GDN_HEADS = 8
GDN_D = 128
RET_HEADS = 4
RET_DK = 256
RET_DV = 512
ROPE_BASE = 10000.0
EPS = 1e-6

META_ROWS = 128
GDN_CHUNK = 64
LANES = 128
BF16_SUBLANES = 16

F32 = jnp.float32
BF16 = jnp.bfloat16


def _sigmoid(x):
    return 1.0 / (1.0 + jnp.exp(-x))


def _silu(x):
    return x * _sigmoid(x)


def _dot(a, b):
    return jnp.dot(a, b, preferred_element_type=F32)


def _dot_nt(a, b):
    return lax.dot_general(a, b, (((1,), (1,)), ((), ())), preferred_element_type=F32)


def _dot_tn(a, b):
    return lax.dot_general(a, b, (((0,), (0,)), ((), ())), preferred_element_type=F32)


def _inproj_kernel(h_ref, gain_ref, w_ref, wab_ref, proj_ref, ab_ref, xn_ref):
    @pl.when(pl.program_id(1) == 0)
    def _():
        x = h_ref[...]
        ms = jnp.mean(x * x, axis=-1, keepdims=True)
        xn = (x * lax.rsqrt(ms + EPS) * gain_ref[...]).astype(BF16)
        xn_ref[...] = xn
        ab_ref[...] = _dot(xn, wab_ref[...])

    proj_ref[...] = _dot(xn_ref[...], w_ref[...]).astype(proj_ref.dtype)


def _inproj(h2, gain, w, wab, *, tm, tn):
    m, d = h2.shape
    n = w.shape[1]
    nab = wab.shape[1]
    return pl.pallas_call(
        _inproj_kernel,
        grid=(m // tm, n // tn),
        in_specs=[
            pl.BlockSpec((tm, d), lambda i, j: (i, 0)),
            pl.BlockSpec((1, d), lambda i, j: (0, 0)),
            pl.BlockSpec((d, tn), lambda i, j: (0, j)),
            pl.BlockSpec((d, nab), lambda i, j: (0, 0)),
        ],
        out_specs=[
            pl.BlockSpec((tm, tn), lambda i, j: (i, j)),
            pl.BlockSpec((tm, nab), lambda i, j: (i, 0)),
        ],
        out_shape=[
            jax.ShapeDtypeStruct((m, n), BF16),
            jax.ShapeDtypeStruct((m, nab), F32),
        ],
        scratch_shapes=[pltpu.VMEM((tm, d), BF16)],
        compiler_params=pltpu.CompilerParams(dimension_semantics=("parallel", "arbitrary")),
        name="inproj",
    )(h2, gain, w, wab)


def _gdn_prep_kernel(main_ref, prev_ref, next_ref, ab_ref, convw_ref, adt_ref, qkv_ref, gb_ref,
                     *, tp, chunk, seq_real):
    i = pl.program_id(1)
    nab = ab_ref.shape[-1]
    half = nab // 2

    ab = ab_ref[0]
    a_log = adt_ref[0:1, :]
    dt = adt_ref[1:2, :]
    xs = ab + dt
    softplus = jnp.maximum(xs, 0.0) + jnp.log1p(jnp.exp(-jnp.abs(xs)))
    g = -jnp.exp(a_log) * softplus
    beta = _sigmoid(ab)
    row = i * tp + lax.broadcasted_iota(jnp.int32, (tp, 1), 0)
    valid = (row < seq_real) | (row >= seq_real + META_ROWS - N_META)
    g = jnp.where(valid, g, 0.0)
    beta = jnp.where(valid, beta, 0.0)

    shift = int(math.log2(chunk))
    ri = lax.broadcasted_iota(jnp.int32, (tp, tp), 0)
    ci = lax.broadcasted_iota(jnp.int32, (tp, tp), 1)
    same = jnp.right_shift(ri, shift) == jnp.right_shift(ci, shift)
    lower = (same & (ci <= ri)).astype(F32)
    upper = (same & (ci >= ri)).astype(F32)
    gc_fwd = jnp.dot(lower, g, precision=lax.Precision.HIGHEST, preferred_element_type=F32)
    gc_bwd = jnp.dot(upper, g, precision=lax.Precision.HIGHEST, preferred_element_type=F32)
    lane = lax.broadcasted_iota(jnp.int32, (tp, nab), 1)
    gc = jnp.where(lane < half // 2, gc_fwd, gc_bwd)
    gb_ref[0] = jnp.where(lane < half, gc, beta)

    rows = lax.broadcasted_iota(jnp.int32, (tp, 1), 0)
    pad = (CONV_K - 1) // 2
    n_groups = main_ref.shape[-1] // LANES
    qk_groups = 2 * GDN_HEADS
    for gidx in range(n_groups):
        cs = slice(gidx * LANES, (gidx + 1) * LANES)
        xm = main_ref[0, :, cs].astype(F32)
        xp = prev_ref[0, :, cs].astype(F32)
        xn = next_ref[0, :, cs].astype(F32)
        hp = prev_ref.shape[1]
        y = jnp.zeros((tp, LANES), F32)
        for t in range(CONV_K):
            s = t - pad
            if s == 0:
                xs_t = xm
            else:
                xs_t = pltpu.roll(xm, (-s) % tp, 0)
                if s < 0:
                    for e in range(-s):
                        src = hp + s + e
                        xs_t = jnp.where(rows == e, xp[src:src + 1, :], xs_t)
                else:
                    for e in range(s):
                        xs_t = jnp.where(rows == tp - s + e, xn[e:e + 1, :], xs_t)
            y = y + convw_ref[t:t + 1, cs] * xs_t
        y = _silu(y)
        if gidx < qk_groups:
            y = y * lax.rsqrt(jnp.sum(y * y, axis=-1, keepdims=True) + EPS)
            if gidx < GDN_HEADS:
                y = y * (GDN_D ** -0.5)
        qkv_ref[0, :, cs] = y.astype(qkv_ref.dtype)


def _gdn_prep(proj, ab, conv_w, adt, *, tp, chunk, seq_real):
    b, s, _ = proj.shape
    cdim = conv_w.shape[1]
    nab = ab.shape[-1]
    nhb = s // BF16_SUBLANES
    per = tp // BF16_SUBLANES
    kern = functools.partial(_gdn_prep_kernel, tp=tp, chunk=chunk, seq_real=seq_real)
    return pl.pallas_call(
        kern,
        grid=(b, s // tp),
        in_specs=[
            pl.BlockSpec((1, tp, cdim), lambda bi, i: (bi, i, 0)),
            pl.BlockSpec((1, BF16_SUBLANES, cdim), lambda bi, i: (bi, (i * per + nhb - 1) % nhb, 0)),
            pl.BlockSpec((1, BF16_SUBLANES, cdim), lambda bi, i: (bi, ((i + 1) * per) % nhb, 0)),
            pl.BlockSpec((1, tp, nab), lambda bi, i: (bi, i, 0)),
            pl.BlockSpec((CONV_K, cdim), lambda bi, i: (0, 0)),
            pl.BlockSpec((2, nab), lambda bi, i: (0, 0)),
        ],
        out_specs=[
            pl.BlockSpec((1, tp, cdim), lambda bi, i: (bi, i, 0)),
            pl.BlockSpec((1, tp, nab), lambda bi, i: (bi, i, 0)),
        ],
        out_shape=[
            jax.ShapeDtypeStruct((b, s, cdim), BF16),
            jax.ShapeDtypeStruct((b, s, nab), F32),
        ],
        compiler_params=pltpu.CompilerParams(dimension_semantics=("parallel", "parallel")),
        name="gdn_prep",
    )(proj, proj, proj, ab, conv_w, adt)


def _gdn_unit(q, k, v, gcc, gcr, beta, s_ref, hh, reverse):
    c = q.shape[0]
    kf = k.astype(F32)
    e = jnp.exp(gcc)
    g_last = gcc[0:1] if reverse else gcc[c - 1:c]
    kb = kf * beta
    kbg = (kb * e).astype(BF16)
    vb = (v.astype(F32) * beta).astype(BF16)
    qd = (q.astype(F32) * e).astype(BF16)
    kd = (kf * jnp.exp(g_last - gcc)).astype(BF16)

    ri = lax.broadcasted_iota(jnp.int32, (c, c), 0)
    ci = lax.broadcasted_iota(jnp.int32, (c, c), 1)
    incl = (ri <= ci) if reverse else (ri >= ci)
    strict = (ri < ci) if reverse else (ri > ci)
    decay = jnp.where(incl, jnp.exp(jnp.where(incl, gcc - gcr, 0.0)), 0.0)

    kk = _dot_nt(jnp.concatenate([kb.astype(BF16), q], axis=0), k)
    m = jnp.where(strict, kk[:c] * decay, 0.0)
    qk = (kk[c:] * decay).astype(BF16)

    p = jnp.where(ri == ci, 1.0, 0.0) - m
    mp = m
    for _ in range(int(math.log2(c)) - 1):
        mpb = mp.astype(BF16)
        mp = _dot(mpb, mpb)
        p = p + _dot(p.astype(BF16), mp.astype(BF16))

    uw = _dot(p.astype(BF16), jnp.concatenate([vb, kbg], axis=1))
    u = uw[:, :GDN_D]
    w = uw[:, GDN_D:]

    s = s_ref[hh]
    ws = _dot(jnp.concatenate([w.astype(BF16), qd], axis=0), s.astype(BF16))
    v_new = (u - ws[:c]).astype(BF16)
    o = ws[c:] + _dot(qk, v_new)
    s_ref[hh] = s * jnp.exp(g_last) + _dot_tn(kd, v_new)
    return o


def _gdn_scan_kernel(qf_ref, qb_ref, gcf_ref, gcb_ref, grf_ref, grb_ref, of_ref, ob_ref,
                     sf_ref, sb_ref, *, chunk):
    @pl.when(pl.program_id(1) == 0)
    def _():
        sf_ref[...] = jnp.zeros_like(sf_ref)
        sb_ref[...] = jnp.zeros_like(sb_ref)

    rows = of_ref.shape[1]
    n_chunks = rows // chunk
    qk_w = GDN_HEADS * GDN_D
    dirs = ((qf_ref, gcf_ref, grf_ref, of_ref, sf_ref), (qb_ref, gcb_ref, grb_ref, ob_ref, sb_ref))
    for d, (q_ref, gc_ref, gr_ref, o_ref, s_ref) in enumerate(dirs):
        reverse = d == 1
        order = range(n_chunks - 1, -1, -1) if reverse else range(n_chunks)
        for ch in order:
            rs = slice(ch * chunk, (ch + 1) * chunk)
            for hh in range(GDN_HEADS):
                col = d * GDN_HEADS + hh
                hs = slice(hh * GDN_D, (hh + 1) * GDN_D)
                q = q_ref[0, rs, hs]
                k = q_ref[0, rs, qk_w + hh * GDN_D:qk_w + (hh + 1) * GDN_D]
                v = q_ref[0, rs, 2 * qk_w + hh * GDN_D:2 * qk_w + (hh + 1) * GDN_D]
                gcc = gc_ref[0, rs, col:col + 1]
                beta = gc_ref[0, rs, 2 * GDN_HEADS + col:2 * GDN_HEADS + col + 1]
                gcr = gr_ref[0, col:col + 1, rs]
                o = _gdn_unit(q, k, v, gcc, gcr, beta, s_ref, hh, reverse)
                o_ref[0, rs, hs] = o.astype(o_ref.dtype)


def _gdn_scan(qkv, gb, gbt, *, chunk):
    b, s, cdim = qkv.shape
    nab = gb.shape[-1]
    r = META_ROWS
    nb = s // r
    width = GDN_HEADS * GDN_D
    fwd = lambda bi, l: (bi, (l + nb - 1) % nb, 0)
    bwd = lambda bi, l: (bi, (2 * nb - 2 - l) % nb, 0)
    fwd_t = lambda bi, l: (bi, 0, (l + nb - 1) % nb)
    bwd_t = lambda bi, l: (bi, 0, (2 * nb - 2 - l) % nb)
    return pl.pallas_call(
        functools.partial(_gdn_scan_kernel, chunk=chunk),
        grid=(b, nb),
        in_specs=[
            pl.BlockSpec((1, r, cdim), fwd),
            pl.BlockSpec((1, r, cdim), bwd),
            pl.BlockSpec((1, r, nab), fwd),
            pl.BlockSpec((1, r, nab), bwd),
            pl.BlockSpec((1, nab, r), fwd_t),
            pl.BlockSpec((1, nab, r), bwd_t),
        ],
        out_specs=[
            pl.BlockSpec((1, r, width), fwd),
            pl.BlockSpec((1, r, width), bwd),
        ],
        out_shape=[
            jax.ShapeDtypeStruct((b, s, width), BF16),
            jax.ShapeDtypeStruct((b, s, width), BF16),
        ],
        scratch_shapes=[
            pltpu.VMEM((GDN_HEADS, GDN_D, GDN_D), F32),
            pltpu.VMEM((GDN_HEADS, GDN_D, GDN_D), F32),
        ],
        compiler_params=pltpu.CompilerParams(dimension_semantics=("parallel", "arbitrary")),
        name="gdn_scan",
    )(qkv, qkv, gb, gb, gbt, gbt)


def _rope(t, cos, sin):
    half = t.shape[-1] // 2
    t1 = t[:, :half]
    t2 = t[:, half:]
    return jnp.concatenate([t1 * cos - t2 * sin, t1 * sin + t2 * cos], axis=-1)


def _ret_scan_kernel(qf_ref, kf_ref, vf_ref, csf_ref, snf_ref, qb_ref, kb_ref, vb_ref, csb_ref, snb_ref,
                     logit_ref, of_ref, ob_ref, rf_ref, rb_ref):
    @pl.when(pl.program_id(1) == 0)
    def _():
        rf_ref[...] = jnp.zeros_like(rf_ref)
        rb_ref[...] = jnp.zeros_like(rb_ref)

    c = of_ref.shape[1]
    ri = lax.broadcasted_iota(jnp.int32, (c, c), 0)
    ci = lax.broadcasted_iota(jnp.int32, (c, c), 1)
    pos = lax.broadcasted_iota(jnp.int32, (c, 1), 0).astype(F32)
    logit = logit_ref[...]
    log_gamma = jnp.minimum(logit, 0.0) - jnp.log1p(jnp.exp(-jnp.abs(logit)))

    dirs = ((qf_ref, kf_ref, vf_ref, csf_ref, snf_ref, of_ref, rf_ref),
            (qb_ref, kb_ref, vb_ref, csb_ref, snb_ref, ob_ref, rb_ref))
    for d, (q_ref, k_ref, v_ref, cs_ref, sn_ref, o_ref, r_ref) in enumerate(dirs):
        reverse = d == 1
        cos = cs_ref[...]
        sin = sn_ref[...]
        for hh in range(RET_HEADS):
            lg = log_gamma[d:d + 1, hh:hh + 1]
            if reverse:
                incl = ri <= ci
                rel = (ci - ri).astype(F32)
                q_pow = c - pos
                k_pow = pos
            else:
                incl = ri >= ci
                rel = (ri - ci).astype(F32)
                q_pow = pos + 1.0
                k_pow = c - 1.0 - pos
            intra = jnp.where(incl, jnp.exp(jnp.where(incl, rel, 0.0) * lg), 0.0)
            qs = slice(hh * RET_DK, (hh + 1) * RET_DK)
            vs = slice(hh * RET_DV, (hh + 1) * RET_DV)
            qr = _rope(q_ref[0, :, qs].astype(F32), cos, sin)
            kr = _rope(k_ref[0, :, qs].astype(F32), cos, sin) * (RET_DK ** -0.5)
            v = v_ref[0, :, vs]
            qk = (_dot_nt(qr.astype(BF16), kr.astype(BF16)) * intra).astype(BF16)
            q_dec = (qr * jnp.exp(lg * q_pow)).astype(BF16)
            k_dec = (kr * jnp.exp(lg * k_pow)).astype(BF16)
            r = r_ref[hh]
            o = _dot(q_dec, r.astype(BF16)) + _dot(qk, v)
            r_ref[hh] = r * jnp.exp(lg * float(c)) + _dot_tn(k_dec, v)
            o_ref[0, :, vs] = o.astype(o_ref.dtype)


def _ret_scan(proj, cos, sin, logit):
    b, s, _ = proj.shape
    r = META_ROWS
    nb = s // r
    qw = RET_HEADS * RET_DK
    vw = RET_HEADS * RET_DV
    q_blk, k_blk, v_blk = 4, 5, 3

    def spec(width, blk, rot):
        return pl.BlockSpec((1, r, width), lambda bi, l: (bi, rot(l), blk))

    fwd = lambda l: (l + nb - 1) % nb
    bwd = lambda l: (2 * nb - 2 - l) % nb
    tab = lambda rot: pl.BlockSpec((r, cos.shape[1]), lambda bi, l: (rot(l), 0))
    out = lambda rot: pl.BlockSpec((1, r, vw), lambda bi, l: (bi, rot(l), 0))
    return pl.pallas_call(
        _ret_scan_kernel,
        grid=(b, nb),
        in_specs=[
            spec(qw, q_blk, fwd), spec(qw, k_blk, fwd), spec(vw, v_blk, fwd), tab(fwd), tab(fwd),
            spec(qw, q_blk, bwd), spec(qw, k_blk, bwd), spec(vw, v_blk, bwd), tab(bwd), tab(bwd),
            pl.BlockSpec(logit.shape, lambda bi, l: (0, 0)),
        ],
        out_specs=[out(fwd), out(bwd)],
        out_shape=[
            jax.ShapeDtypeStruct((b, s, vw), BF16),
            jax.ShapeDtypeStruct((b, s, vw), BF16),
        ],
        scratch_shapes=[
            pltpu.VMEM((RET_HEADS, RET_DK, RET_DV), F32),
            pltpu.VMEM((RET_HEADS, RET_DK, RET_DV), F32),
        ],
        compiler_params=pltpu.CompilerParams(dimension_semantics=("parallel", "arbitrary")),
        name="ret_scan",
    )(proj, proj, proj, cos, sin, proj, proj, proj, cos, sin, logit)


def _group_rms(x, width):
    parts = []
    for gidx in range(x.shape[-1] // width):
        xs = x[:, gidx * width:(gidx + 1) * width]
        parts.append(xs * lax.rsqrt(jnp.mean(xs * xs, axis=-1, keepdims=True) + EPS))
    return jnp.concatenate(parts, axis=-1)


def _mix_kernel(of_ref, ob_ref, z_ref, rf_ref, rb_ref, g_ref, ga_ref, gb_ref, h_ref,
                gain_ref, wa_ref, wb_ref, wo_ref, out_ref):
    oa = of_ref[...].astype(F32) + ob_ref[...].astype(F32)
    xa = _group_rms(oa, GDN_D) * gain_ref[...] * _silu(z_ref[...].astype(F32))
    ya = _dot(xa.astype(BF16), wa_ref[...])
    orr = rf_ref[...].astype(F32) + rb_ref[...].astype(F32)
    xb = _group_rms(orr, RET_DV) * _silu(g_ref[...].astype(F32))
    yb = _dot(xb.astype(BF16), wb_ref[...])
    merged = _sigmoid(ga_ref[...].astype(F32)) * ya + _sigmoid(gb_ref[...].astype(F32)) * yb
    out_ref[...] = h_ref[...] + _dot(merged.astype(BF16), wo_ref[...])


def _mix(o_f, o_b, r_f, r_b, proj2, h2, gain_t, wa, wb, wo, *, tm):
    m, d = h2.shape
    va = o_f.shape[1]
    vb = r_f.shape[1]
    row = lambda width, blk: pl.BlockSpec((tm, width), lambda i: (i, blk))
    full = lambda a: pl.BlockSpec(a.shape, lambda i: (0, 0))
    return pl.pallas_call(
        _mix_kernel,
        grid=(m // tm,),
        in_specs=[
            row(va, 0), row(va, 0), row(va, 3), row(vb, 0), row(vb, 0), row(vb, 4), row(d, 10), row(d, 11),
            row(d, 0), full(gain_t), full(wa), full(wb), full(wo),
        ],
        out_specs=row(d, 0),
        out_shape=jax.ShapeDtypeStruct((m, d), F32),
        compiler_params=pltpu.CompilerParams(dimension_semantics=("parallel",)),
        name="mix_out",
    )(o_f, o_b, proj2, r_f, r_b, proj2, proj2, proj2, h2, gain_t, wa, wb, wo)


def _ffn_kernel(h_ref, gain_ref, wi_ref, wo_ref, out_ref):
    x = h_ref[...]
    ms = jnp.mean(x * x, axis=-1, keepdims=True)
    xn = (x * lax.rsqrt(ms + EPS) * gain_ref[...]).astype(BF16)
    gu = _dot(xn, wi_ref[...])
    hid = gu.shape[1] // 2
    act = (_silu(gu[:, :hid]) * gu[:, hid:]).astype(BF16)
    out_ref[...] = x + _dot(act, wo_ref[...])


def _ffn(h2, gain, wi, wo, *, tm):
    m, d = h2.shape
    full = lambda a: pl.BlockSpec(a.shape, lambda i: (0, 0))
    return pl.pallas_call(
        _ffn_kernel,
        grid=(m // tm,),
        in_specs=[pl.BlockSpec((tm, d), lambda i: (i, 0)), full(gain), full(wi), full(wo)],
        out_specs=pl.BlockSpec((tm, d), lambda i: (i, 0)),
        out_shape=jax.ShapeDtypeStruct((m, d), F32),
        compiler_params=pltpu.CompilerParams(dimension_semantics=("parallel",)),
        name="ffn",
    )(h2, gain, wi, wo)


def _final_kernel(h_ref, gain_ref, out_ref):
    x = h_ref[...]
    ms = jnp.mean(x * x, axis=-1, keepdims=True)
    out_ref[...] = x * lax.rsqrt(ms + EPS) * gain_ref[...]


def _final_norm(h, gain, seq_real, *, tr):
    b, _, d = h.shape
    return pl.pallas_call(
        _final_kernel,
        grid=(b, seq_real // tr),
        in_specs=[pl.BlockSpec((1, tr, d), lambda bi, i: (bi, i, 0)), pl.BlockSpec((1, d), lambda bi, i: (0, 0))],
        out_specs=pl.BlockSpec((1, tr, d), lambda bi, i: (bi, i, 0)),
        out_shape=jax.ShapeDtypeStruct((b, seq_real, d), F32),
        compiler_params=pltpu.CompilerParams(dimension_semantics=("parallel", "parallel")),
        name="final_norm",
    )(h, gain)


def _tile(total, want):
    t = min(want, total)
    while total % t:
        t //= 2
    return t


def kernel(x, meta_tokens, norm_mix, w_in, conv_w, gdn_a_log, gdn_dt_bias, gdn_norm, ret_decay_logit,
           w_up_a, w_up_b, w_out, norm_ffn, w_ffn_in, w_ffn_out, norm_final):
    b, l, d = x.shape
    depth = w_in.shape[0]
    qk_a = GDN_HEADS * GDN_D
    conv_dim = 3 * qk_a
    n_ab = 4 * GDN_HEADS
    assert d == qk_a == RET_HEADS * RET_DK and conv_w.shape[-1] == conv_dim
    assert l % META_ROWS == 0 and META_ROWS % GDN_CHUNK == 0
    s = l + META_ROWS
    m = b * s

    meta_blk = jnp.concatenate([jnp.zeros((META_ROWS - N_META, d), x.dtype), meta_tokens.astype(x.dtype)], axis=0)
    h = jnp.concatenate([x, jnp.broadcast_to(meta_blk[None], (b, META_ROWS, d))], axis=1)

    half = RET_DK // 2
    row = jnp.arange(s)
    pos = jnp.where(row < l, row + N_META, jnp.maximum(row - l - (META_ROWS - N_META), 0))
    inv = ROPE_BASE ** (-jnp.arange(half, dtype=F32) / half)
    ang = pos.astype(F32)[:, None] * inv[None, :]
    cos, sin = jnp.cos(ang), jnp.sin(ang)

    tm_proj = _tile(m, 1024)
    tm_dense = _tile(m, 256)
    tp = _tile(s, 384) if s % 384 == 0 else META_ROWS
    h2 = h.reshape(m, d)
    for i in range(depth):
        wi = w_in[i]
        w_main = jnp.concatenate([wi[:, :conv_dim + qk_a], wi[:, conv_dim + qk_a + n_ab:]], axis=1).astype(BF16)
        w_ab = wi[:, conv_dim + qk_a:conv_dim + qk_a + n_ab].astype(BF16)
        proj2, ab2 = _inproj(h2, norm_mix[i][None], w_main, w_ab, tm=tm_proj, tn=w_main.shape[1] // 8)
        proj = proj2.reshape(b, s, -1)

        adt = jnp.stack([jnp.pad(gdn_a_log[i].reshape(-1), (0, n_ab // 2)),
                         jnp.pad(gdn_dt_bias[i].reshape(-1), (0, n_ab // 2))]).astype(F32)
        qkv, gb = _gdn_prep(proj, ab2.reshape(b, s, n_ab), conv_w[i].astype(F32), adt,
                            tp=tp, chunk=GDN_CHUNK, seq_real=l)
        o_f, o_b = _gdn_scan(qkv, gb, jnp.swapaxes(gb, 1, 2), chunk=GDN_CHUNK)
        r_f, r_b = _ret_scan(proj, cos, sin, ret_decay_logit[i].astype(F32))

        gain_t = jnp.tile(gdn_norm[i].astype(F32), GDN_HEADS)[None]
        h2 = _mix(o_f.reshape(m, -1), o_b.reshape(m, -1), r_f.reshape(m, -1), r_b.reshape(m, -1), proj2, h2,
                  gain_t, w_up_a[i].astype(BF16), w_up_b[i].astype(BF16), w_out[i].astype(BF16), tm=tm_dense)
        h2 = _ffn(h2, norm_ffn[i][None], w_ffn_in[i].astype(BF16), w_ffn_out[i].astype(BF16), tm=tm_dense)

    return _final_norm(h2.reshape(b, s, d), norm_final[None], l, tr=_tile(l, 512))
```

```python
import functools
import math

import jax
import jax.numpy as jnp
from jax import lax
from jax.experimental import pallas as pl
from jax.experimental.pallas import tpu as pltpu

N_META = 16
CONV_K = 5
GDN_HEADS = 8
GDN_D = 128
RET_HEADS = 4
RET_DK = 256
RET_DV = 512
ROPE_BASE = 10000.0
EPS = 1e-6

META_ROWS = 128
GDN_CHUNK = 64
LANES = 128
BF16_SUBLANES = 16

F32 = jnp.float32
BF16 = jnp.bfloat16


def _sigmoid(x):
    return 1.0 / (1.0 + jnp.exp(-x))


def _silu(x):
    return x * _sigmoid(x)


def _dot(a, b):
    return jnp.dot(a, b, preferred_element_type=F32)


def _dot_nt(a, b):
    return lax.dot_general(a, b, (((1,), (1,)), ((), ())), preferred_element_type=F32)


def _dot_tn(a, b):
    return lax.dot_general(a, b, (((0,), (0,)), ((), ())), preferred_element_type=F32)


def _inproj_kernel(h_ref, gain_ref, w_ref, wab_ref, proj_ref, ab_ref, xn_ref):
    @pl.when(pl.program_id(1) == 0)
    def _():
        x = h_ref[...]
        ms = jnp.mean(x * x, axis=-1, keepdims=True)
        xn = (x * lax.rsqrt(ms + EPS) * gain_ref[...]).astype(BF16)
        xn_ref[...] = xn
        ab_ref[...] = _dot(xn, wab_ref[...])

    proj_ref[...] = _dot(xn_ref[...], w_ref[...]).astype(proj_ref.dtype)


def _inproj(h2, gain, w, wab, *, tm, tn):
    m, d = h2.shape
    n = w.shape[1]
    nab = wab.shape[1]
    return pl.pallas_call(
        _inproj_kernel,
        grid=(m // tm, n // tn),
        in_specs=[
            pl.BlockSpec((tm, d), lambda i, j: (i, 0)),
            pl.BlockSpec((1, d), lambda i, j: (0, 0)),
            pl.BlockSpec((d, tn), lambda i, j: (0, j)),
            pl.BlockSpec((d, nab), lambda i, j: (0, 0)),
        ],
        out_specs=[
            pl.BlockSpec((tm, tn), lambda i, j: (i, j)),
            pl.BlockSpec((tm, nab), lambda i, j: (i, 0)),
        ],
        out_shape=[
            jax.ShapeDtypeStruct((m, n), BF16),
            jax.ShapeDtypeStruct((m, nab), F32),
        ],
        scratch_shapes=[pltpu.VMEM((tm, d), BF16)],
        compiler_params=pltpu.CompilerParams(dimension_semantics=("parallel", "arbitrary")),
        name="inproj",
    )(h2, gain, w, wab)


def _gdn_prep_kernel(main_ref, prev_ref, next_ref, ab_ref, convw_ref, adt_ref, qkv_ref, kt_ref, gb_ref,
                     *, tp, chunk, seq_real):
    i = pl.program_id(1)
    nab = ab_ref.shape[-1]
    half = nab // 2

    ab = ab_ref[0]
    a_log = adt_ref[0:1, :]
    dt = adt_ref[1:2, :]
    xs = ab + dt
    softplus = jnp.maximum(xs, 0.0) + jnp.log1p(jnp.exp(-jnp.abs(xs)))
    g = -jnp.exp(a_log) * softplus
    beta = _sigmoid(ab)
    row = i * tp + lax.broadcasted_iota(jnp.int32, (tp, 1), 0)
    valid = (row < seq_real) | (row >= seq_real + META_ROWS - N_META)
    g = jnp.where(valid, g, 0.0)
    beta = jnp.where(valid, beta, 0.0)

    shift = int(math.log2(chunk))
    ri = lax.broadcasted_iota(jnp.int32, (tp, tp), 0)
    ci = lax.broadcasted_iota(jnp.int32, (tp, tp), 1)
    same = jnp.right_shift(ri, shift) == jnp.right_shift(ci, shift)
    lower = (same & (ci <= ri)).astype(F32)
    upper = (same & (ci >= ri)).astype(F32)
    gc_fwd = jnp.dot(lower, g, precision=lax.Precision.HIGHEST, preferred_element_type=F32)
    gc_bwd = jnp.dot(upper, g, precision=lax.Precision.HIGHEST, preferred_element_type=F32)
    lane = lax.broadcasted_iota(jnp.int32, (tp, nab), 1)
    gc = jnp.where(lane < half // 2, gc_fwd, gc_bwd)
    gb_ref[0] = jnp.where(lane < half, gc, beta)

    rows = lax.broadcasted_iota(jnp.int32, (tp, 1), 0)
    pad = (CONV_K - 1) // 2
    n_groups = main_ref.shape[-1] // LANES
    qk_groups = 2 * GDN_HEADS
    for gidx in range(n_groups):
        cs = slice(gidx * LANES, (gidx + 1) * LANES)
        xm = main_ref[0, :, cs].astype(F32)
        xp = prev_ref[0, :, cs].astype(F32)
        xn = next_ref[0, :, cs].astype(F32)
        hp = prev_ref.shape[1]
        y = jnp.zeros((tp, LANES), F32)
        for t in range(CONV_K):
            s = t - pad
            if s == 0:
                xs_t = xm
            else:
                xs_t = pltpu.roll(xm, (-s) % tp, 0)
                if s < 0:
                    for e in range(-s):
                        src = hp + s + e
                        xs_t = jnp.where(rows == e, xp[src:src + 1, :], xs_t)
                else:
                    for e in range(s):
                        xs_t = jnp.where(rows == tp - s + e, xn[e:e + 1, :], xs_t)
            y = y + convw_ref[t:t + 1, cs] * xs_t
        y = _silu(y)
        if gidx < qk_groups:
            y = y * lax.rsqrt(jnp.sum(y * y, axis=-1, keepdims=True) + EPS)
            if gidx < GDN_HEADS:
                y = y * (GDN_D ** -0.5)
        qkv_ref[0, gidx] = y.astype(qkv_ref.dtype)
        if GDN_HEADS <= gidx < qk_groups:
            kt_ref[0, gidx - GDN_HEADS] = y.T.astype(kt_ref.dtype)


def _gdn_prep(proj, ab, conv_w, adt, *, tp, chunk, seq_real):
    b, s, _ = proj.shape
    cdim = conv_w.shape[1]
    nab = ab.shape[-1]
    nhb = s // BF16_SUBLANES
    per = tp // BF16_SUBLANES
    kern = functools.partial(_gdn_prep_kernel, tp=tp, chunk=chunk, seq_real=seq_real)
    return pl.pallas_call(
        kern,
        grid=(b, s // tp),
        in_specs=[
            pl.BlockSpec((1, tp, cdim), lambda bi, i: (bi, i, 0)),
            pl.BlockSpec((1, BF16_SUBLANES, cdim), lambda bi, i: (bi, (i * per + nhb - 1) % nhb, 0)),
            pl.BlockSpec((1, BF16_SUBLANES, cdim), lambda bi, i: (bi, ((i + 1) * per) % nhb, 0)),
            pl.BlockSpec((1, tp, nab), lambda bi, i: (bi, i, 0)),
            pl.BlockSpec((CONV_K, cdim), lambda bi, i: (0, 0)),
            pl.BlockSpec((2, nab), lambda bi, i: (0, 0)),
        ],
        out_specs=[
            pl.BlockSpec((1, 3 * GDN_HEADS, tp, GDN_D), lambda bi, i: (bi, 0, i, 0)),
            pl.BlockSpec((1, GDN_HEADS, GDN_D, tp), lambda bi, i: (bi, 0, 0, i)),
            pl.BlockSpec((1, tp, nab), lambda bi, i: (bi, i, 0)),
        ],
        out_shape=[
            jax.ShapeDtypeStruct((b, 3 * GDN_HEADS, s, GDN_D), BF16),
            jax.ShapeDtypeStruct((b, GDN_HEADS, GDN_D, s), BF16),
            jax.ShapeDtypeStruct((b, s, nab), F32),
        ],
        compiler_params=pltpu.CompilerParams(dimension_semantics=("parallel", "parallel")),
        name="gdn_prep",
    )(proj, proj, proj, ab, conv_w, adt)


def _bdot(a, b):
    return lax.dot_general(a, b, (((2,), (1,)), ((0,), (0,))), preferred_element_type=F32)


def _gdn_scan_kernel(qf_ref, qb_ref, ktf_ref, ktb_ref, gcf_ref, gcb_ref, grf_ref, grb_ref, of_ref, ob_ref,
                     s_ref, *, chunk):
    @pl.when(pl.program_id(1) == 0)
    def _():
        s_ref[...] = jnp.zeros_like(s_ref)

    nh = GDN_HEADS
    c = chunk
    n_steps = of_ref.shape[1] // c
    dirs = ((qf_ref, ktf_ref, gcf_ref, grf_ref), (qb_ref, ktb_ref, gcb_ref, grb_ref))
    order = (tuple(range(n_steps)), tuple(range(n_steps - 1, -1, -1)))

    qs, ks, vs, kts, gccs, gcrs, betas, glasts = [], [], [], [], [], [], [], []
    for d, (q_ref, kt_ref, gc_ref, gr_ref) in enumerate(dirs):
        for ch in order[d]:
            rs = slice(ch * c, (ch + 1) * c)
            qs.append(q_ref[0, 0:nh, rs, :])
            ks.append(q_ref[0, nh:2 * nh, rs, :])
            vs.append(q_ref[0, 2 * nh:3 * nh, rs, :])
            kts.append(kt_ref[0, :, :, rs])
            gcol = gc_ref[0, rs, :]
            grow = gr_ref[0, d * nh:(d + 1) * nh, rs]
            gcc = jnp.stack([gcol[:, d * nh + h:d * nh + h + 1] for h in range(nh)], axis=0)
            gccs.append(gcc)
            betas.append(jnp.stack([gcol[:, (2 + d) * nh + h:(2 + d) * nh + h + 1] for h in range(nh)], axis=0))
            gcrs.append(jnp.stack([grow[h:h + 1, :] for h in range(nh)], axis=0))
            glasts.append(gcc[:, 0:1, :] if d == 1 else gcc[:, c - 1:c, :])
    cat = lambda xs: jnp.concatenate(xs, axis=0)
    q, k, v, kt = cat(qs), cat(ks), cat(vs), cat(kts)
    gcc, gcr, beta, g_last = cat(gccs), cat(gcrs), cat(betas), cat(glasts)
    n_units = q.shape[0]

    e = jnp.exp(gcc)
    kb = k.astype(F32) * beta
    kbg = (kb * e).astype(BF16)
    vb = (v.astype(F32) * beta).astype(BF16)
    qd = (q.astype(F32) * e).astype(BF16)
    kdt = (kt.astype(F32) * jnp.exp(g_last - gcr)).astype(BF16)

    ui = lax.broadcasted_iota(jnp.int32, (n_units, c, c), 0)
    ri = lax.broadcasted_iota(jnp.int32, (n_units, c, c), 1)
    ci = lax.broadcasted_iota(jnp.int32, (n_units, c, c), 2)
    incl = jnp.where(ui >= n_units // 2, ci - ri, ri - ci) >= 0
    decay = jnp.where(incl, jnp.exp(jnp.where(incl, gcc - gcr, 0.0)), 0.0)

    kk = _bdot(jnp.concatenate([kb.astype(BF16), q], axis=1), kt)
    m = jnp.where(ri == ci, 0.0, kk[:, :c] * decay)
    qk = (kk[:, c:] * decay).astype(BF16)

    p = jnp.where(ri == ci, 1.0, 0.0) - m
    mp = m
    for _ in range(int(math.log2(c)) - 1):
        mpb = mp.astype(BF16)
        mp = _bdot(mpb, mpb)
        p = p + _bdot(p.astype(BF16), mp.astype(BF16))

    uw = _bdot(p.astype(BF16), jnp.concatenate([vb, kbg], axis=2))
    u = uw[:, :, :GDN_D]
    wq = jnp.concatenate([uw[:, :, GDN_D:].astype(BF16), qd], axis=1)
    eg = jnp.exp(g_last)

    s = s_ref[...]
    for j in range(n_steps):
        pick = lambda a: jnp.concatenate([a[j * nh:(j + 1) * nh], a[(n_steps + j) * nh:(n_steps + j + 1) * nh]],
                                         axis=0)
        ws = _bdot(pick(wq), s.astype(BF16))
        v_new = (pick(u) - ws[:, :c]).astype(BF16)
        o = ws[:, c:] + _bdot(pick(qk), v_new)
        s = s * pick(eg) + _bdot(pick(kdt), v_new)
        for d, o_ref in enumerate((of_ref, ob_ref)):
            ch = order[d][j]
            for h in range(nh):
                o_ref[0, ch * c:(ch + 1) * c, h * GDN_D:(h + 1) * GDN_D] = o[d * nh + h].astype(o_ref.dtype)
    s_ref[...] = s


def _gdn_scan(qkv, kt, gb, gbt, *, chunk):
    b, n3, s, _ = qkv.shape
    nab = gb.shape[-1]
    r = META_ROWS
    nb = s // r
    width = GDN_HEADS * GDN_D
    fwd = lambda l: (l + nb - 1) % nb
    bwd = lambda l: (2 * nb - 2 - l) % nb
    qspec = lambda rot: pl.BlockSpec((1, n3, r, GDN_D), lambda bi, l: (bi, 0, rot(l), 0))
    ktspec = lambda rot: pl.BlockSpec((1, GDN_HEADS, GDN_D, r), lambda bi, l: (bi, 0, 0, rot(l)))
    cspec = lambda rot: pl.BlockSpec((1, r, nab), lambda bi, l: (bi, rot(l), 0))
    rspec = lambda rot: pl.BlockSpec((1, nab, r), lambda bi, l: (bi, 0, rot(l)))
    ospec = lambda rot: pl.BlockSpec((1, r, width), lambda bi, l: (bi, rot(l), 0))
    return pl.pallas_call(
        functools.partial(_gdn_scan_kernel, chunk=chunk),
        grid=(b, nb),
        in_specs=[qspec(fwd), qspec(bwd), ktspec(fwd), ktspec(bwd), cspec(fwd), cspec(bwd), rspec(fwd), rspec(bwd)],
        out_specs=[ospec(fwd), ospec(bwd)],
        out_shape=[
            jax.ShapeDtypeStruct((b, s, width), BF16),
            jax.ShapeDtypeStruct((b, s, width), BF16),
        ],
        scratch_shapes=[pltpu.VMEM((2 * GDN_HEADS, GDN_D, GDN_D), F32)],
        compiler_params=pltpu.CompilerParams(dimension_semantics=("parallel", "arbitrary")),
        name="gdn_scan",
    )(qkv, qkv, kt, kt, gb, gb, gbt, gbt)


def _rope(t, cos, sin):
    half = t.shape[-1] // 2
    t1 = t[:, :half]
    t2 = t[:, half:]
    return jnp.concatenate([t1 * cos - t2 * sin, t1 * sin + t2 * cos], axis=-1)


def _ret_scan_kernel(qf_ref, kf_ref, vf_ref, csf_ref, snf_ref, qb_ref, kb_ref, vb_ref, csb_ref, snb_ref,
                     logit_ref, of_ref, ob_ref, rf_ref, rb_ref):
    @pl.when(pl.program_id(1) == 0)
    def _():
        rf_ref[...] = jnp.zeros_like(rf_ref)
        rb_ref[...] = jnp.zeros_like(rb_ref)

    c = of_ref.shape[1]
    ri = lax.broadcasted_iota(jnp.int32, (c, c), 0)
    ci = lax.broadcasted_iota(jnp.int32, (c, c), 1)
    pos = lax.broadcasted_iota(jnp.int32, (c, 1), 0).astype(F32)
    logit = logit_ref[...]
    log_gamma = jnp.minimum(logit, 0.0) - jnp.log1p(jnp.exp(-jnp.abs(logit)))

    dirs = ((qf_ref, kf_ref, vf_ref, csf_ref, snf_ref, of_ref, rf_ref),
            (qb_ref, kb_ref, vb_ref, csb_ref, snb_ref, ob_ref, rb_ref))
    for d, (q_ref, k_ref, v_ref, cs_ref, sn_ref, o_ref, r_ref) in enumerate(dirs):
        reverse = d == 1
        cos = cs_ref[...]
        sin = sn_ref[...]
        for hh in range(RET_HEADS):
            lg = log_gamma[d:d + 1, hh:hh + 1]
            if reverse:
                incl = ri <= ci
                rel = (ci - ri).astype(F32)
                q_pow = c - pos
                k_pow = pos
            else:
                incl = ri >= ci
                rel = (ri - ci).astype(F32)
                q_pow = pos + 1.0
                k_pow = c - 1.0 - pos
            intra = jnp.where(incl, jnp.exp(jnp.where(incl, rel, 0.0) * lg), 0.0)
            qs = slice(hh * RET_DK, (hh + 1) * RET_DK)
            vs = slice(hh * RET_DV, (hh + 1) * RET_DV)
            qr = _rope(q_ref[0, :, qs].astype(F32), cos, sin)
            kr = _rope(k_ref[0, :, qs].astype(F32), cos, sin) * (RET_DK ** -0.5)
            v = v_ref[0, :, vs]
            qk = (_dot_nt(qr.astype(BF16), kr.astype(BF16)) * intra).astype(BF16)
            q_dec = (qr * jnp.exp(lg * q_pow)).astype(BF16)
            k_dec = (kr * jnp.exp(lg * k_pow)).astype(BF16)
            r = r_ref[hh]
            o = _dot(q_dec, r.astype(BF16)) + _dot(qk, v)
            r_ref[hh] = r * jnp.exp(lg * float(c)) + _dot_tn(k_dec, v)
            o_ref[0, :, vs] = o.astype(o_ref.dtype)


def _ret_scan(proj, cos, sin, logit):
    b, s, _ = proj.shape
    r = META_ROWS
    nb = s // r
    qw = RET_HEADS * RET_DK
    vw = RET_HEADS * RET_DV
    q_blk, k_blk, v_blk = 4, 5, 3

    def spec(width, blk, rot):
        return pl.BlockSpec((1, r, width), lambda bi, l: (bi, rot(l), blk))

    fwd = lambda l: (l + nb - 1) % nb
    bwd = lambda l: (2 * nb - 2 - l) % nb
    tab = lambda rot: pl.BlockSpec((r, cos.shape[1]), lambda bi, l: (rot(l), 0))
    out = lambda rot: pl.BlockSpec((1, r, vw), lambda bi, l: (bi, rot(l), 0))
    return pl.pallas_call(
        _ret_scan_kernel,
        grid=(b, nb),
        in_specs=[
            spec(qw, q_blk, fwd), spec(qw, k_blk, fwd), spec(vw, v_blk, fwd), tab(fwd), tab(fwd),
            spec(qw, q_blk, bwd), spec(qw, k_blk, bwd), spec(vw, v_blk, bwd), tab(bwd), tab(bwd),
            pl.BlockSpec(logit.shape, lambda bi, l: (0, 0)),
        ],
        out_specs=[out(fwd), out(bwd)],
        out_shape=[
            jax.ShapeDtypeStruct((b, s, vw), BF16),
            jax.ShapeDtypeStruct((b, s, vw), BF16),
        ],
        scratch_shapes=[
            pltpu.VMEM((RET_HEADS, RET_DK, RET_DV), F32),
            pltpu.VMEM((RET_HEADS, RET_DK, RET_DV), F32),
        ],
        compiler_params=pltpu.CompilerParams(dimension_semantics=("parallel", "arbitrary")),
        name="ret_scan",
    )(proj, proj, proj, cos, sin, proj, proj, proj, cos, sin, logit)


def _group_rms(x, width):
    parts = []
    for gidx in range(x.shape[-1] // width):
        xs = x[:, gidx * width:(gidx + 1) * width]
        parts.append(xs * lax.rsqrt(jnp.mean(xs * xs, axis=-1, keepdims=True) + EPS))
    return jnp.concatenate(parts, axis=-1)


def _mix_kernel(of_ref, ob_ref, z_ref, rf_ref, rb_ref, g_ref, ga_ref, gb_ref, h_ref,
                gain_ref, wa_ref, wb_ref, wo_ref, out_ref):
    oa = of_ref[...].astype(F32) + ob_ref[...].astype(F32)
    xa = _group_rms(oa, GDN_D) * gain_ref[...] * _silu(z_ref[...].astype(F32))
    ya = _dot(xa.astype(BF16), wa_ref[...])
    orr = rf_ref[...].astype(F32) + rb_ref[...].astype(F32)
    xb = _group_rms(orr, RET_DV) * _silu(g_ref[...].astype(F32))
    yb = _dot(xb.astype(BF16), wb_ref[...])
    merged = _sigmoid(ga_ref[...].astype(F32)) * ya + _sigmoid(gb_ref[...].astype(F32)) * yb
    out_ref[...] = h_ref[...] + _dot(merged.astype(BF16), wo_ref[...])


def _mix(o_f, o_b, r_f, r_b, proj2, h2, gain_t, wa, wb, wo, *, tm):
    m, d = h2.shape
    va = o_f.shape[1]
    vb = r_f.shape[1]
    row = lambda width, blk: pl.BlockSpec((tm, width), lambda i: (i, blk))
    full = lambda a: pl.BlockSpec(a.shape, lambda i: (0, 0))
    return pl.pallas_call(
        _mix_kernel,
        grid=(m // tm,),
        in_specs=[
            row(va, 0), row(va, 0), row(va, 3), row(vb, 0), row(vb, 0), row(vb, 4), row(d, 10), row(d, 11),
            row(d, 0), full(gain_t), full(wa), full(wb), full(wo),
        ],
        out_specs=row(d, 0),
        out_shape=jax.ShapeDtypeStruct((m, d), F32),
        compiler_params=pltpu.CompilerParams(dimension_semantics=("parallel",)),
        name="mix_out",
    )(o_f, o_b, proj2, r_f, r_b, proj2, proj2, proj2, h2, gain_t, wa, wb, wo)


def _ffn_kernel(h_ref, gain_ref, wi_ref, wo_ref, out_ref):
    x = h_ref[...]
    ms = jnp.mean(x * x, axis=-1, keepdims=True)
    xn = (x * lax.rsqrt(ms + EPS) * gain_ref[...]).astype(BF16)
    gu = _dot(xn, wi_ref[...])
    hid = gu.shape[1] // 2
    act = (_silu(gu[:, :hid]) * gu[:, hid:]).astype(BF16)
    out_ref[...] = x + _dot(act, wo_ref[...])


def _ffn(h2, gain, wi, wo, *, tm):
    m, d = h2.shape
    full = lambda a: pl.BlockSpec(a.shape, lambda i: (0, 0))
    return pl.pallas_call(
        _ffn_kernel,
        grid=(m // tm,),
        in_specs=[pl.BlockSpec((tm, d), lambda i: (i, 0)), full(gain), full(wi), full(wo)],
        out_specs=pl.BlockSpec((tm, d), lambda i: (i, 0)),
        out_shape=jax.ShapeDtypeStruct((m, d), F32),
        compiler_params=pltpu.CompilerParams(dimension_semantics=("parallel",)),
        name="ffn",
    )(h2, gain, wi, wo)


def _final_kernel(h_ref, gain_ref, out_ref):
    x = h_ref[...]
    ms = jnp.mean(x * x, axis=-1, keepdims=True)
    out_ref[...] = x * lax.rsqrt(ms + EPS) * gain_ref[...]


def _final_norm(h, gain, seq_real, *, tr):
    b, _, d = h.shape
    return pl.pallas_call(
        _final_kernel,
        grid=(b, seq_real // tr),
        in_specs=[pl.BlockSpec((1, tr, d), lambda bi, i: (bi, i, 0)), pl.BlockSpec((1, d), lambda bi, i: (0, 0))],
        out_specs=pl.BlockSpec((1, tr, d), lambda bi, i: (bi, i, 0)),
        out_shape=jax.ShapeDtypeStruct((b, seq_real, d), F32),
        compiler_params=pltpu.CompilerParams(dimension_semantics=("parallel", "parallel")),
        name="final_norm",
    )(h, gain)


def _tile(total, want):
    t = min(want, total)
    while total % t:
        t //= 2
    return t


def kernel(x, meta_tokens, norm_mix, w_in, conv_w, gdn_a_log, gdn_dt_bias, gdn_norm, ret_decay_logit,
           w_up_a, w_up_b, w_out, norm_ffn, w_ffn_in, w_ffn_out, norm_final):
    b, l, d = x.shape
    depth = w_in.shape[0]
    qk_a = GDN_HEADS * GDN_D
    conv_dim = 3 * qk_a
    n_ab = 4 * GDN_HEADS
    assert d == qk_a == RET_HEADS * RET_DK and conv_w.shape[-1] == conv_dim
    assert l % META_ROWS == 0 and META_ROWS % GDN_CHUNK == 0
    s = l + META_ROWS
    m = b * s

    meta_blk = jnp.concatenate([jnp.zeros((META_ROWS - N_META, d), x.dtype), meta_tokens.astype(x.dtype)], axis=0)
    h = jnp.concatenate([x, jnp.broadcast_to(meta_blk[None], (b, META_ROWS, d))], axis=1)

    half = RET_DK // 2
    row = jnp.arange(s)
    pos = jnp.where(row < l, row + N_META, jnp.maximum(row - l - (META_ROWS - N_META), 0))
    inv = ROPE_BASE ** (-jnp.arange(half, dtype=F32) / half)
    ang = pos.astype(F32)[:, None] * inv[None, :]
    cos, sin = jnp.cos(ang), jnp.sin(ang)

    tm_proj = _tile(m, 1024)
    tm_dense = _tile(m, 256)
    tp = _tile(s, 384) if s % 384 == 0 else META_ROWS
    h2 = h.reshape(m, d)
    for i in range(depth):
        wi = w_in[i]
        w_main = jnp.concatenate([wi[:, :conv_dim + qk_a], wi[:, conv_dim + qk_a + n_ab:]], axis=1).astype(BF16)
        w_ab = wi[:, conv_dim + qk_a:conv_dim + qk_a + n_ab].astype(BF16)
        proj2, ab2 = _inproj(h2, norm_mix[i][None], w_main, w_ab, tm=tm_proj, tn=w_main.shape[1] // 8)
        proj = proj2.reshape(b, s, -1)

        adt = jnp.stack([jnp.pad(gdn_a_log[i].reshape(-1), (0, n_ab // 2)),
                         jnp.pad(gdn_dt_bias[i].reshape(-1), (0, n_ab // 2))]).astype(F32)
        qkv, kt, gb = _gdn_prep(proj, ab2.reshape(b, s, n_ab), conv_w[i].astype(F32), adt,
                                tp=tp, chunk=GDN_CHUNK, seq_real=l)
        o_f, o_b = _gdn_scan(qkv, kt, gb, jnp.swapaxes(gb, 1, 2), chunk=GDN_CHUNK)
        r_f, r_b = _ret_scan(proj, cos, sin, ret_decay_logit[i].astype(F32))

        gain_t = jnp.tile(gdn_norm[i].astype(F32), GDN_HEADS)[None]
        h2 = _mix(o_f.reshape(m, -1), o_b.reshape(m, -1), r_f.reshape(m, -1), r_b.reshape(m, -1), proj2, h2,
                  gain_t, w_up_a[i].astype(BF16), w_up_b[i].astype(BF16), w_out[i].astype(BF16), tm=tm_dense)
        h2 = _ffn(h2, norm_ffn[i][None], w_ffn_in[i].astype(BF16), w_ffn_out[i].astype(BF16), tm=tm_dense)

    return _final_norm(h2.reshape(b, s, d), norm_final[None], l, tr=_tile(l, 512))
```

```python
import functools
import math

import jax
import jax.numpy as jnp
from jax import lax
from jax.experimental import pallas as pl
from jax.experimental.pallas import tpu as pltpu

N_META = 16
CONV_K = 5
GDN_HEADS = 8
GDN_D = 128
RET_HEADS = 4
RET_DK = 256
RET_DV = 512
ROPE_BASE = 10000.0
EPS = 1e-6

META_ROWS = 256
GDN_BLOCK = 128
GDN_CHUNK = 64
RET_CHUNK = 256
LANES = 128
BF16_SUBLANES = 16

F32 = jnp.float32
BF16 = jnp.bfloat16


def _sigmoid(x):
    return 0.5 * jnp.tanh(0.5 * x) + 0.5


def _silu(x):
    h = 0.5 * x
    return h * jnp.tanh(h) + h


def _dot(a, b):
    return jnp.dot(a, b, preferred_element_type=F32)


def _bdot(a, b):
    return lax.dot_general(a, b, (((2,), (1,)), ((0,), (0,))), preferred_element_type=F32)


def _bdot_nt(a, b):
    return lax.dot_general(a, b, (((2,), (2,)), ((0,), (0,))), preferred_element_type=F32)


def _bdot_tn(a, b):
    return lax.dot_general(a, b, (((1,), (1,)), ((0,), (0,))), preferred_element_type=F32)


def _inproj_kernel(h_ref, gain_ref, w_ref, wab_ref, proj_ref, ab_ref, xn_ref):
    @pl.when(pl.program_id(1) == 0)
    def _():
        x = h_ref[...]
        ms = jnp.mean(x * x, axis=-1, keepdims=True)
        xn = (x * lax.rsqrt(ms + EPS) * gain_ref[...]).astype(BF16)
        xn_ref[...] = xn
        ab_ref[...] = _dot(xn, wab_ref[...])

    proj_ref[...] = _dot(xn_ref[...], w_ref[...]).astype(proj_ref.dtype)


def _inproj(h2, gain, w, wab, *, tm, tn):
    m, d = h2.shape
    n = w.shape[1]
    nab = wab.shape[1]
    return pl.pallas_call(
        _inproj_kernel,
        grid=(m // tm, n // tn),
        in_specs=[
            pl.BlockSpec((tm, d), lambda i, j: (i, 0)),
            pl.BlockSpec((1, d), lambda i, j: (0, 0)),
            pl.BlockSpec((d, tn), lambda i, j: (0, j)),
            pl.BlockSpec((d, nab), lambda i, j: (0, 0)),
        ],
        out_specs=[
            pl.BlockSpec((tm, tn), lambda i, j: (i, j)),
            pl.BlockSpec((tm, nab), lambda i, j: (i, 0)),
        ],
        out_shape=[
            jax.ShapeDtypeStruct((m, n), BF16),
            jax.ShapeDtypeStruct((m, nab), F32),
        ],
        scratch_shapes=[pltpu.VMEM((tm, d), BF16)],
        compiler_params=pltpu.CompilerParams(dimension_semantics=("parallel", "arbitrary")),
        name="inproj",
    )(h2, gain, w, wab)


def _gdn_prep_kernel(main_ref, prev_ref, next_ref, ab_ref, convw_ref, adt_ref, qkv_ref, kt_ref, gb_ref,
                     xe_ref, *, tp, chunk, seq_real):
    i = pl.program_id(1)
    nab = ab_ref.shape[-1]
    half = nab // 2

    ab = ab_ref[0]
    a_log = adt_ref[0:1, :]
    dt = adt_ref[1:2, :]
    xs = ab + dt
    softplus = jnp.maximum(xs, 0.0) + jnp.log1p(jnp.exp(-jnp.abs(xs)))
    g = -jnp.exp(a_log) * softplus
    beta = _sigmoid(ab)
    row = i * tp + lax.broadcasted_iota(jnp.int32, (tp, 1), 0)
    valid = (row < seq_real) | (row >= seq_real + META_ROWS - N_META)
    g = jnp.where(valid, g, 0.0)
    beta = jnp.where(valid, beta, 0.0)

    shift = int(math.log2(chunk))
    ri = lax.broadcasted_iota(jnp.int32, (tp, tp), 0)
    ci = lax.broadcasted_iota(jnp.int32, (tp, tp), 1)
    same = jnp.right_shift(ri, shift) == jnp.right_shift(ci, shift)
    lower = (same & (ci <= ri)).astype(F32)
    upper = (same & (ci >= ri)).astype(F32)
    gc_fwd = jnp.dot(lower, g, precision=lax.Precision.HIGHEST, preferred_element_type=F32)
    gc_bwd = jnp.dot(upper, g, precision=lax.Precision.HIGHEST, preferred_element_type=F32)
    lane = lax.broadcasted_iota(jnp.int32, (tp, nab), 1)
    gc = jnp.where(lane < half // 2, gc_fwd, gc_bwd)
    gb_ref[0] = jnp.where(lane < half, gc, beta)

    pad = (CONV_K - 1) // 2
    halo = 8
    n_groups = main_ref.shape[-1] // LANES
    qk_groups = 2 * GDN_HEADS
    hp = prev_ref.shape[1]
    for gidx in range(n_groups):
        cs = slice(gidx * LANES, (gidx + 1) * LANES)
        xe_ref[0:halo] = prev_ref[0, :, cs].astype(F32)[hp - halo:]
        xe_ref[halo:halo + tp] = main_ref[0, :, cs].astype(F32)
        xe_ref[halo + tp:] = next_ref[0, :, cs].astype(F32)[:halo]
        y = jnp.zeros((tp, LANES), F32)
        for t in range(CONV_K):
            y = y + convw_ref[t:t + 1, cs] * xe_ref[halo + t - pad:halo + t - pad + tp]
        y = _silu(y)
        if gidx < qk_groups:
            y = y * lax.rsqrt(jnp.sum(y * y, axis=-1, keepdims=True) + EPS)
            if gidx < GDN_HEADS:
                y = y * (GDN_D ** -0.5)
        qkv_ref[0, gidx] = y.astype(qkv_ref.dtype)
        if GDN_HEADS <= gidx < qk_groups:
            kt_ref[0, gidx - GDN_HEADS] = y.T.astype(kt_ref.dtype)


def _gdn_prep(proj, ab, conv_w, adt, *, tp, chunk, seq_real):
    b, s, _ = proj.shape
    cdim = conv_w.shape[1]
    nab = ab.shape[-1]
    nhb = s // BF16_SUBLANES
    per = tp // BF16_SUBLANES
    kern = functools.partial(_gdn_prep_kernel, tp=tp, chunk=chunk, seq_real=seq_real)
    return pl.pallas_call(
        kern,
        grid=(b, s // tp),
        in_specs=[
            pl.BlockSpec((1, tp, cdim), lambda bi, i: (bi, i, 0)),
            pl.BlockSpec((1, BF16_SUBLANES, cdim), lambda bi, i: (bi, (i * per + nhb - 1) % nhb, 0)),
            pl.BlockSpec((1, BF16_SUBLANES, cdim), lambda bi, i: (bi, ((i + 1) * per) % nhb, 0)),
            pl.BlockSpec((1, tp, nab), lambda bi, i: (bi, i, 0)),
            pl.BlockSpec((CONV_K, cdim), lambda bi, i: (0, 0)),
            pl.BlockSpec((2, nab), lambda bi, i: (0, 0)),
        ],
        out_specs=[
            pl.BlockSpec((1, 3 * GDN_HEADS, tp, GDN_D), lambda bi, i: (bi, 0, i, 0)),
            pl.BlockSpec((1, GDN_HEADS, GDN_D, tp), lambda bi, i: (bi, 0, 0, i)),
            pl.BlockSpec((1, tp, nab), lambda bi, i: (bi, i, 0)),
        ],
        out_shape=[
            jax.ShapeDtypeStruct((b, 3 * GDN_HEADS, s, GDN_D), BF16),
            jax.ShapeDtypeStruct((b, GDN_HEADS, GDN_D, s), BF16),
            jax.ShapeDtypeStruct((b, s, nab), F32),
        ],
        scratch_shapes=[pltpu.VMEM((tp + 16, LANES), F32)],
        compiler_params=pltpu.CompilerParams(dimension_semantics=("parallel", "parallel")),
        name="gdn_prep",
    )(proj, proj, proj, ab, conv_w, adt)


def _gdn_scan_kernel(qf_ref, qb_ref, ktf_ref, ktb_ref, gcf_ref, gcb_ref, grf_ref, grb_ref, of_ref, ob_ref,
                     s_ref, *, chunk):
    @pl.when(pl.program_id(1) == 0)
    def _():
        s_ref[...] = jnp.zeros_like(s_ref)

    nh = GDN_HEADS
    c = chunk
    n_steps = of_ref.shape[1] // c
    dirs = ((qf_ref, ktf_ref, gcf_ref, grf_ref), (qb_ref, ktb_ref, gcb_ref, grb_ref))
    order = (tuple(range(n_steps)), tuple(range(n_steps - 1, -1, -1)))

    qs, ks, vs, kts, gccs, gcrs, betas, glasts = [], [], [], [], [], [], [], []
    for d, (q_ref, kt_ref, gc_ref, gr_ref) in enumerate(dirs):
        for ch in order[d]:
            rs = slice(ch * c, (ch + 1) * c)
            qs.append(q_ref[0, 0:nh, rs, :])
            ks.append(q_ref[0, nh:2 * nh, rs, :])
            vs.append(q_ref[0, 2 * nh:3 * nh, rs, :])
            kts.append(kt_ref[0, :, :, rs])
            gcol = gc_ref[0, rs, :]
            grow = gr_ref[0, d * nh:(d + 1) * nh, rs]
            gcc = jnp.stack([gcol[:, d * nh + h:d * nh + h + 1] for h in range(nh)], axis=0)
            gccs.append(gcc)
            betas.append(jnp.stack([gcol[:, (2 + d) * nh + h:(2 + d) * nh + h + 1] for h in range(nh)], axis=0))
            gcrs.append(jnp.stack([grow[h:h + 1, :] for h in range(nh)], axis=0))
            glasts.append(gcc[:, 0:1, :] if d == 1 else gcc[:, c - 1:c, :])
    cat = lambda xs: jnp.concatenate(xs, axis=0)
    q, k, v, kt = cat(qs), cat(ks), cat(vs), cat(kts)
    gcc, gcr, beta, g_last = cat(gccs), cat(gcrs), cat(betas), cat(glasts)
    n_units = q.shape[0]

    e = jnp.exp(gcc)
    kb = k.astype(F32) * beta
    kbg = (kb * e).astype(BF16)
    vb = (v.astype(F32) * beta).astype(BF16)
    qd = (q.astype(F32) * e).astype(BF16)
    kdt = (kt.astype(F32) * jnp.exp(g_last - gcr)).astype(BF16)

    ui = lax.broadcasted_iota(jnp.int32, (n_units, c, c), 0)
    ri = lax.broadcasted_iota(jnp.int32, (n_units, c, c), 1)
    ci = lax.broadcasted_iota(jnp.int32, (n_units, c, c), 2)
    incl = jnp.where(ui >= n_units // 2, ci - ri, ri - ci) >= 0
    decay = jnp.where(incl, jnp.exp(jnp.where(incl, gcc - gcr, 0.0)), 0.0)

    kk = _bdot(jnp.concatenate([kb.astype(BF16), q], axis=1), kt)
    m = jnp.where(ri == ci, 0.0, kk[:, :c] * decay)
    qk = (kk[:, c:] * decay).astype(BF16)

    p = jnp.where(ri == ci, 1.0, 0.0) - m
    mp = m
    for _ in range(int(math.log2(c)) - 1):
        mpb = mp.astype(BF16)
        mp = _bdot(mpb, mpb)
        p = p + _bdot(p.astype(BF16), mp.astype(BF16))

    uw = _bdot(p.astype(BF16), jnp.concatenate([vb, kbg], axis=2))
    u = uw[:, :, :GDN_D]
    wq = jnp.concatenate([uw[:, :, GDN_D:].astype(BF16), qd], axis=1)
    eg = jnp.exp(g_last)

    s = s_ref[...]
    for j in range(n_steps):
        pick = lambda a: jnp.concatenate([a[j * nh:(j + 1) * nh], a[(n_steps + j) * nh:(n_steps + j + 1) * nh]],
                                         axis=0)
        ws = _bdot(pick(wq), s.astype(BF16))
        v_new = (pick(u) - ws[:, :c]).astype(BF16)
        o = ws[:, c:] + _bdot(pick(qk), v_new)
        s = s * pick(eg) + _bdot(pick(kdt), v_new)
        for d, o_ref in enumerate((of_ref, ob_ref)):
            ch = order[d][j]
            for h in range(nh):
                o_ref[0, ch * c:(ch + 1) * c, h * GDN_D:(h + 1) * GDN_D] = o[d * nh + h].astype(o_ref.dtype)
    s_ref[...] = s


def _gdn_scan(qkv, kt, gb, gbt, *, chunk):
    b, n3, s, _ = qkv.shape
    nab = gb.shape[-1]
    r = GDN_BLOCK
    nb = s // r
    mb = META_ROWS // r
    width = GDN_HEADS * GDN_D
    fwd = lambda l: (l + nb - mb) % nb
    bwd = lambda l: (2 * nb - 1 - mb - l) % nb
    qspec = lambda rot: pl.BlockSpec((1, n3, r, GDN_D), lambda bi, l: (bi, 0, rot(l), 0))
    ktspec = lambda rot: pl.BlockSpec((1, GDN_HEADS, GDN_D, r), lambda bi, l: (bi, 0, 0, rot(l)))
    cspec = lambda rot: pl.BlockSpec((1, r, nab), lambda bi, l: (bi, rot(l), 0))
    rspec = lambda rot: pl.BlockSpec((1, nab, r), lambda bi, l: (bi, 0, rot(l)))
    ospec = lambda rot: pl.BlockSpec((1, r, width), lambda bi, l: (bi, rot(l), 0))
    return pl.pallas_call(
        functools.partial(_gdn_scan_kernel, chunk=chunk),
        grid=(b, nb),
        in_specs=[qspec(fwd), qspec(bwd), ktspec(fwd), ktspec(bwd), cspec(fwd), cspec(bwd), rspec(fwd), rspec(bwd)],
        out_specs=[ospec(fwd), ospec(bwd)],
        out_shape=[
            jax.ShapeDtypeStruct((b, s, width), BF16),
            jax.ShapeDtypeStruct((b, s, width), BF16),
        ],
        scratch_shapes=[pltpu.VMEM((2 * GDN_HEADS, GDN_D, GDN_D), F32)],
        compiler_params=pltpu.CompilerParams(dimension_semantics=("parallel", "arbitrary")),
        name="gdn_scan",
    )(qkv, qkv, kt, kt, gb, gb, gbt, gbt)


def _rope(t, cos, sin):
    half = t.shape[-1] // 2
    t1 = t[:, :half]
    t2 = t[:, half:]
    return jnp.concatenate([t1 * cos - t2 * sin, t1 * sin + t2 * cos], axis=-1)


def _ret_scan_kernel(qf_ref, kf_ref, vf_ref, csf_ref, snf_ref, qb_ref, kb_ref, vb_ref, csb_ref, snb_ref,
                     logit_ref, of_ref, ob_ref, r_ref, intra_ref, qpow_ref, kpow_ref):
    c = of_ref.shape[1]
    nh = RET_HEADS
    logit = logit_ref[...]
    log_gamma = jnp.minimum(logit, 0.0) - jnp.log1p(jnp.exp(-jnp.abs(logit)))

    @pl.when(pl.program_id(1) == 0)
    def _():
        r_ref[...] = jnp.zeros_like(r_ref)
        ri = lax.broadcasted_iota(jnp.int32, (c, c), 0)
        ci = lax.broadcasted_iota(jnp.int32, (c, c), 1)
        pos = lax.broadcasted_iota(jnp.int32, (c, 1), 0).astype(F32)
        for d in range(2):
            for hh in range(nh):
                lg = log_gamma[d:d + 1, hh:hh + 1]
                if d == 1:
                    rel = ci - ri
                    q_pow = c - pos
                    k_pow = pos
                else:
                    rel = ri - ci
                    q_pow = pos + 1.0
                    k_pow = c - 1.0 - pos
                incl = rel >= 0
                intra_ref[d * nh + hh] = jnp.where(incl, jnp.exp(jnp.where(incl, rel, 0).astype(F32) * lg), 0.0)
                qpow_ref[d * nh + hh] = jnp.exp(lg * q_pow)
                kpow_ref[d * nh + hh] = jnp.exp(lg * k_pow)

    qrs, krs, vs, cds = [], [], [], []
    dirs = ((qf_ref, kf_ref, vf_ref, csf_ref, snf_ref), (qb_ref, kb_ref, vb_ref, csb_ref, snb_ref))
    for d, (q_ref, k_ref, v_ref, cs_ref, sn_ref) in enumerate(dirs):
        cos = cs_ref[...]
        sin = sn_ref[...]
        for hh in range(nh):
            qs = slice(hh * RET_DK, (hh + 1) * RET_DK)
            qrs.append(_rope(q_ref[0, :, qs].astype(F32), cos, sin))
            krs.append(_rope(k_ref[0, :, qs].astype(F32), cos, sin) * (RET_DK ** -0.5))
            vs.append(v_ref[0, :, hh * RET_DV:(hh + 1) * RET_DV])
            cds.append(jnp.exp(log_gamma[d:d + 1, hh:hh + 1] * float(c)))
    qr = jnp.stack(qrs, axis=0)
    kr = jnp.stack(krs, axis=0)
    v = jnp.stack(vs, axis=0)
    chunk_decay = jnp.stack(cds, axis=0)

    qk = (_bdot_nt(qr.astype(BF16), kr.astype(BF16)) * intra_ref[...]).astype(BF16)
    q_dec = (qr * qpow_ref[...]).astype(BF16)
    k_dec = (kr * kpow_ref[...]).astype(BF16)
    r = r_ref[...]
    o = _bdot(q_dec, r.astype(BF16)) + _bdot(qk, v)
    r_ref[...] = r * chunk_decay + _bdot_tn(k_dec, v)
    for d, o_ref in enumerate((of_ref, ob_ref)):
        for hh in range(nh):
            o_ref[0, :, hh * RET_DV:(hh + 1) * RET_DV] = o[d * nh + hh].astype(o_ref.dtype)


def _ret_scan(proj, cos, sin, logit):
    b, s, _ = proj.shape
    r = RET_CHUNK
    nb = s // r
    mb = META_ROWS // r
    qw = RET_HEADS * RET_DK
    vw = RET_HEADS * RET_DV
    n_units = 2 * RET_HEADS
    q_blk, k_blk, v_blk = 4, 5, 3

    def spec(width, blk, rot):
        return pl.BlockSpec((1, r, width), lambda bi, l: (bi, rot(l), blk))

    fwd = lambda l: (l + nb - mb) % nb
    bwd = lambda l: (2 * nb - 1 - mb - l) % nb
    tab = lambda rot: pl.BlockSpec((r, cos.shape[1]), lambda bi, l: (rot(l), 0))
    out = lambda rot: pl.BlockSpec((1, r, vw), lambda bi, l: (bi, rot(l), 0))
    return pl.pallas_call(
        _ret_scan_kernel,
        grid=(b, nb),
        in_specs=[
            spec(qw, q_blk, fwd), spec(qw, k_blk, fwd), spec(vw, v_blk, fwd), tab(fwd), tab(fwd),
            spec(qw, q_blk, bwd), spec(qw, k_blk, bwd), spec(vw, v_blk, bwd), tab(bwd), tab(bwd),
            pl.BlockSpec(logit.shape, lambda bi, l: (0, 0)),
        ],
        out_specs=[out(fwd), out(bwd)],
        out_shape=[
            jax.ShapeDtypeStruct((b, s, vw), BF16),
            jax.ShapeDtypeStruct((b, s, vw), BF16),
        ],
        scratch_shapes=[
            pltpu.VMEM((n_units, RET_DK, RET_DV), F32),
            pltpu.VMEM((n_units, r, r), F32),
            pltpu.VMEM((n_units, r, 1), F32),
            pltpu.VMEM((n_units, r, 1), F32),
        ],
        compiler_params=pltpu.CompilerParams(dimension_semantics=("parallel", "arbitrary")),
        name="ret_scan",
    )(proj, proj, proj, cos, sin, proj, proj, proj, cos, sin, logit)


def _group_rms(x, width):
    parts = []
    for gidx in range(x.shape[-1] // width):
        xs = x[:, gidx * width:(gidx + 1) * width]
        parts.append(xs * lax.rsqrt(jnp.mean(xs * xs, axis=-1, keepdims=True) + EPS))
    return jnp.concatenate(parts, axis=-1)


def _mix_kernel(of_ref, ob_ref, z_ref, rf_ref, rb_ref, g_ref, ga_ref, gb_ref, h_ref,
                gain_ref, wa_ref, wb_ref, wo_ref, out_ref):
    oa = of_ref[...].astype(F32) + ob_ref[...].astype(F32)
    xa = _group_rms(oa, GDN_D) * gain_ref[...] * _silu(z_ref[...].astype(F32))
    ya = _dot(xa.astype(BF16), wa_ref[...])
    orr = rf_ref[...].astype(F32) + rb_ref[...].astype(F32)
    xb = _group_rms(orr, RET_DV) * _silu(g_ref[...].astype(F32))
    yb = _dot(xb.astype(BF16), wb_ref[...])
    merged = _sigmoid(ga_ref[...].astype(F32)) * ya + _sigmoid(gb_ref[...].astype(F32)) * yb
    out_ref[...] = h_ref[...] + _dot(merged.astype(BF16), wo_ref[...])


def _mix(o_f, o_b, r_f, r_b, proj2, h2, gain_t, wa, wb, wo, *, tm):
    m, d = h2.shape
    va = o_f.shape[1]
    vb = r_f.shape[1]
    row = lambda width, blk: pl.BlockSpec((tm, width), lambda i: (i, blk))
    full = lambda a: pl.BlockSpec(a.shape, lambda i: (0, 0))
    return pl.pallas_call(
        _mix_kernel,
        grid=(m // tm,),
        in_specs=[
            row(va, 0), row(va, 0), row(va, 3), row(vb, 0), row(vb, 0), row(vb, 4), row(d, 10), row(d, 11),
            row(d, 0), full(gain_t), full(wa), full(wb), full(wo),
        ],
        out_specs=row(d, 0),
        out_shape=jax.ShapeDtypeStruct((m, d), F32),
        compiler_params=pltpu.CompilerParams(dimension_semantics=("parallel",)),
        name="mix_out",
    )(o_f, o_b, proj2, r_f, r_b, proj2, proj2, proj2, h2, gain_t, wa, wb, wo)


def _rms_gain(x, gain):
    return x * lax.rsqrt(jnp.mean(x * x, axis=-1, keepdims=True) + EPS) * gain


def _ffn_residual(x, gain, wi_ref, wo_ref):
    gu = _dot(_rms_gain(x, gain).astype(BF16), wi_ref[...])
    hid = gu.shape[1] // 2
    act = (_silu(gu[:, :hid]) * gu[:, hid:]).astype(BF16)
    return x + _dot(act, wo_ref[...])


def _ffn_kernel(h_ref, gain_ref, wi_ref, wo_ref, out_ref):
    out_ref[...] = _ffn_residual(h_ref[...], gain_ref[...], wi_ref, wo_ref)


def _ffn_final_kernel(h_ref, gain_ref, wi_ref, wo_ref, fgain_ref, out_ref):
    out_ref[0] = _rms_gain(_ffn_residual(h_ref[0], gain_ref[...], wi_ref, wo_ref), fgain_ref[...])


def _ffn(h2, gain, wi, wo, *, tm):
    m, d = h2.shape
    full = lambda a: pl.BlockSpec(a.shape, lambda i: (0, 0))
    return pl.pallas_call(
        _ffn_kernel,
        grid=(m // tm,),
        in_specs=[pl.BlockSpec((tm, d), lambda i: (i, 0)), full(gain), full(wi), full(wo)],
        out_specs=pl.BlockSpec((tm, d), lambda i: (i, 0)),
        out_shape=jax.ShapeDtypeStruct((m, d), F32),
        compiler_params=pltpu.CompilerParams(dimension_semantics=("parallel",)),
        name="ffn",
    )(h2, gain, wi, wo)


def _ffn_final(h, gain, wi, wo, final_gain, seq_real, *, tm):
    b, _, d = h.shape
    full = lambda a: pl.BlockSpec(a.shape, lambda bi, i: (0, 0))
    return pl.pallas_call(
        _ffn_final_kernel,
        grid=(b, seq_real // tm),
        in_specs=[pl.BlockSpec((1, tm, d), lambda bi, i: (bi, i, 0)), full(gain), full(wi), full(wo),
                  full(final_gain)],
        out_specs=pl.BlockSpec((1, tm, d), lambda bi, i: (bi, i, 0)),
        out_shape=jax.ShapeDtypeStruct((b, seq_real, d), F32),
        compiler_params=pltpu.CompilerParams(dimension_semantics=("parallel", "parallel")),
        name="ffn_final",
    )(h, gain, wi, wo, final_gain)


def _tile(total, want):
    t = min(want, total)
    while total % t:
        t //= 2
    return t


def kernel(x, meta_tokens, norm_mix, w_in, conv_w, gdn_a_log, gdn_dt_bias, gdn_norm, ret_decay_logit,
           w_up_a, w_up_b, w_out, norm_ffn, w_ffn_in, w_ffn_out, norm_final):
    b, l, d = x.shape
    depth = w_in.shape[0]
    qk_a = GDN_HEADS * GDN_D
    conv_dim = 3 * qk_a
    n_ab = 4 * GDN_HEADS
    assert d == qk_a == RET_HEADS * RET_DK and conv_w.shape[-1] == conv_dim
    assert l % META_ROWS == 0 and META_ROWS % GDN_BLOCK == 0 and GDN_BLOCK % GDN_CHUNK == 0
    assert META_ROWS % RET_CHUNK == 0
    s = l + META_ROWS
    m = b * s

    meta_blk = jnp.concatenate([jnp.zeros((META_ROWS - N_META, d), x.dtype), meta_tokens.astype(x.dtype)], axis=0)
    h = jnp.concatenate([x, jnp.broadcast_to(meta_blk[None], (b, META_ROWS, d))], axis=1)

    half = RET_DK // 2
    row = jnp.arange(s)
    pos = jnp.where(row < l, row + N_META, jnp.maximum(row - l - (META_ROWS - N_META), 0))
    inv = ROPE_BASE ** (-jnp.arange(half, dtype=F32) / half)
    ang = pos.astype(F32)[:, None] * inv[None, :]
    cos, sin = jnp.cos(ang), jnp.sin(ang)

    tm_proj = _tile(m, 1024)
    tm_dense = _tile(m, 256)
    tp = META_ROWS
    h2 = h.reshape(m, d)
    for i in range(depth):
        wi = w_in[i]
        w_main = jnp.concatenate([wi[:, :conv_dim + qk_a], wi[:, conv_dim + qk_a + n_ab:]], axis=1).astype(BF16)
        w_ab = wi[:, conv_dim + qk_a:conv_dim + qk_a + n_ab].astype(BF16)
        proj2, ab2 = _inproj(h2, norm_mix[i][None], w_main, w_ab, tm=tm_proj, tn=w_main.shape[1] // 8)
        proj = proj2.reshape(b, s, -1)

        adt = jnp.stack([jnp.pad(gdn_a_log[i].reshape(-1), (0, n_ab // 2)),
                         jnp.pad(gdn_dt_bias[i].reshape(-1), (0, n_ab // 2))]).astype(F32)
        qkv, kt, gb = _gdn_prep(proj, ab2.reshape(b, s, n_ab), conv_w[i].astype(F32), adt,
                                tp=tp, chunk=GDN_CHUNK, seq_real=l)
        o_f, o_b = _gdn_scan(qkv, kt, gb, jnp.swapaxes(gb, 1, 2), chunk=GDN_CHUNK)
        r_f, r_b = _ret_scan(proj, cos, sin, ret_decay_logit[i].astype(F32))

        gain_t = jnp.tile(gdn_norm[i].astype(F32), GDN_HEADS)[None]
        h2 = _mix(o_f.reshape(m, -1), o_b.reshape(m, -1), r_f.reshape(m, -1), r_b.reshape(m, -1), proj2, h2,
                  gain_t, w_up_a[i].astype(BF16), w_up_b[i].astype(BF16), w_out[i].astype(BF16), tm=tm_dense)
        ffn_w = (norm_ffn[i][None], w_ffn_in[i].astype(BF16), w_ffn_out[i].astype(BF16))
        if i + 1 < depth:
            h2 = _ffn(h2, *ffn_w, tm=tm_dense)
    return _ffn_final(h2.reshape(b, s, d), *ffn_w, norm_final[None], l, tm=_tile(l, 256))
```

```python
import functools
import math

import jax
import jax.numpy as jnp
from jax import lax
from jax.experimental import pallas as pl
from jax.experimental.pallas import tpu as pltpu

N_META = 16
CONV_K = 5
GDN_HEADS = 8
GDN_D = 128
RET_HEADS = 4
RET_DK = 256
RET_DV = 512
ROPE_BASE = 10000.0
EPS = 1e-6

META_ROWS = 256
GDN_BLOCK = 128
GDN_CHUNK = 64
RET_CHUNK = 256
LANES = 128
BF16_SUBLANES = 16

F32 = jnp.float32
BF16 = jnp.bfloat16


def _sigmoid(x):
    return 0.5 * jnp.tanh(0.5 * x) + 0.5


def _silu(x):
    h = 0.5 * x
    return h * jnp.tanh(h) + h


def _dot(a, b):
    return jnp.dot(a, b, preferred_element_type=F32)


def _bdot(a, b):
    return lax.dot_general(a, b, (((2,), (1,)), ((0,), (0,))), preferred_element_type=F32)


def _bdot_nt(a, b):
    return lax.dot_general(a, b, (((2,), (2,)), ((0,), (0,))), preferred_element_type=F32)


def _bdot_tn(a, b):
    return lax.dot_general(a, b, (((1,), (1,)), ((0,), (0,))), preferred_element_type=F32)


def _inproj_kernel(h_ref, gain_ref, w_ref, wab_ref, proj_ref, ab_ref, xn_ref):
    @pl.when(pl.program_id(1) == 0)
    def _():
        x = h_ref[...]
        ms = jnp.mean(x * x, axis=-1, keepdims=True)
        xn = (x * lax.rsqrt(ms + EPS) * gain_ref[...]).astype(BF16)
        xn_ref[...] = xn
        ab_ref[...] = _dot(xn, wab_ref[...])

    proj_ref[...] = _dot(xn_ref[...], w_ref[...]).astype(proj_ref.dtype)


def _inproj(h2, gain, w, wab, *, tm, tn):
    m, d = h2.shape
    n = w.shape[1]
    nab = wab.shape[1]
    return pl.pallas_call(
        _inproj_kernel,
        grid=(m // tm, n // tn),
        in_specs=[
            pl.BlockSpec((tm, d), lambda i, j: (i, 0)),
            pl.BlockSpec((1, d), lambda i, j: (0, 0)),
            pl.BlockSpec((d, tn), lambda i, j: (0, j)),
            pl.BlockSpec((d, nab), lambda i, j: (0, 0)),
        ],
        out_specs=[
            pl.BlockSpec((tm, tn), lambda i, j: (i, j)),
            pl.BlockSpec((tm, nab), lambda i, j: (i, 0)),
        ],
        out_shape=[
            jax.ShapeDtypeStruct((m, n), BF16),
            jax.ShapeDtypeStruct((m, nab), F32),
        ],
        scratch_shapes=[pltpu.VMEM((tm, d), BF16)],
        compiler_params=pltpu.CompilerParams(dimension_semantics=("parallel", "arbitrary")),
        name="inproj",
    )(h2, gain, w, wab)


def _gdn_prep_kernel(main_ref, prev_ref, next_ref, ab_ref, convw_ref, adt_ref, qkv_ref, kt_ref, gb_ref,
                     xe_ref, *, tp, chunk, seq_real):
    i = pl.program_id(1)
    nab = ab_ref.shape[-1]
    half = nab // 2

    ab = ab_ref[0]
    a_log = adt_ref[0:1, :]
    dt = adt_ref[1:2, :]
    xs = ab + dt
    softplus = jnp.maximum(xs, 0.0) + jnp.log1p(jnp.exp(-jnp.abs(xs)))
    g = -jnp.exp(a_log) * softplus
    beta = _sigmoid(ab)
    row = i * tp + lax.broadcasted_iota(jnp.int32, (tp, 1), 0)
    valid = (row < seq_real) | (row >= seq_real + META_ROWS - N_META)
    g = jnp.where(valid, g, 0.0)
    beta = jnp.where(valid, beta, 0.0)

    shift = int(math.log2(chunk))
    ri = lax.broadcasted_iota(jnp.int32, (tp, tp), 0)
    ci = lax.broadcasted_iota(jnp.int32, (tp, tp), 1)
    same = jnp.right_shift(ri, shift) == jnp.right_shift(ci, shift)
    lower = (same & (ci <= ri)).astype(F32)
    upper = (same & (ci >= ri)).astype(F32)
    gc_fwd = jnp.dot(lower, g, precision=lax.Precision.HIGHEST, preferred_element_type=F32)
    gc_bwd = jnp.dot(upper, g, precision=lax.Precision.HIGHEST, preferred_element_type=F32)
    lane = lax.broadcasted_iota(jnp.int32, (tp, nab), 1)
    gc = jnp.where(lane < half // 2, gc_fwd, gc_bwd)
    gb_ref[0] = jnp.where(lane < half, gc, beta)

    pad = (CONV_K - 1) // 2
    halo = 8
    n_groups = main_ref.shape[-1] // LANES
    qk_groups = 2 * GDN_HEADS
    hp = prev_ref.shape[1]
    for gidx in range(n_groups):
        cs = slice(gidx * LANES, (gidx + 1) * LANES)
        xe_ref[0:halo] = prev_ref[0, :, cs].astype(F32)[hp - halo:]
        xe_ref[halo:halo + tp] = main_ref[0, :, cs].astype(F32)
        xe_ref[halo + tp:] = next_ref[0, :, cs].astype(F32)[:halo]
        y = jnp.zeros((tp, LANES), F32)
        for t in range(CONV_K):
            y = y + convw_ref[t:t + 1, cs] * xe_ref[halo + t - pad:halo + t - pad + tp]
        y = _silu(y)
        if gidx < qk_groups:
            y = y * lax.rsqrt(jnp.sum(y * y, axis=-1, keepdims=True) + EPS)
            if gidx < GDN_HEADS:
                y = y * (GDN_D ** -0.5)
        qkv_ref[0, gidx] = y.astype(qkv_ref.dtype)
        if GDN_HEADS <= gidx < qk_groups:
            kt_ref[0, gidx - GDN_HEADS] = y.T.astype(kt_ref.dtype)


def _gdn_prep(proj, ab, conv_w, adt, *, tp, chunk, seq_real):
    b, s, _ = proj.shape
    cdim = conv_w.shape[1]
    nab = ab.shape[-1]
    nhb = s // BF16_SUBLANES
    per = tp // BF16_SUBLANES
    kern = functools.partial(_gdn_prep_kernel, tp=tp, chunk=chunk, seq_real=seq_real)
    return pl.pallas_call(
        kern,
        grid=(b, s // tp),
        in_specs=[
            pl.BlockSpec((1, tp, cdim), lambda bi, i: (bi, i, 0)),
            pl.BlockSpec((1, BF16_SUBLANES, cdim), lambda bi, i: (bi, (i * per + nhb - 1) % nhb, 0)),
            pl.BlockSpec((1, BF16_SUBLANES, cdim), lambda bi, i: (bi, ((i + 1) * per) % nhb, 0)),
            pl.BlockSpec((1, tp, nab), lambda bi, i: (bi, i, 0)),
            pl.BlockSpec((CONV_K, cdim), lambda bi, i: (0, 0)),
            pl.BlockSpec((2, nab), lambda bi, i: (0, 0)),
        ],
        out_specs=[
            pl.BlockSpec((1, 3 * GDN_HEADS, tp, GDN_D), lambda bi, i: (bi, 0, i, 0)),
            pl.BlockSpec((1, GDN_HEADS, GDN_D, tp), lambda bi, i: (bi, 0, 0, i)),
            pl.BlockSpec((1, tp, nab), lambda bi, i: (bi, i, 0)),
        ],
        out_shape=[
            jax.ShapeDtypeStruct((b, 3 * GDN_HEADS, s, GDN_D), BF16),
            jax.ShapeDtypeStruct((b, GDN_HEADS, GDN_D, s), BF16),
            jax.ShapeDtypeStruct((b, s, nab), F32),
        ],
        scratch_shapes=[pltpu.VMEM((tp + 16, LANES), F32)],
        compiler_params=pltpu.CompilerParams(dimension_semantics=("parallel", "parallel")),
        name="gdn_prep",
    )(proj, proj, proj, ab, conv_w, adt)


def _gdn_scan_kernel(qf_ref, qb_ref, ktf_ref, ktb_ref, gcf_ref, gcb_ref, grf_ref, grb_ref, of_ref, ob_ref,
                     s_ref, *, chunk):
    @pl.when(pl.program_id(1) == 0)
    def _():
        s_ref[...] = jnp.zeros_like(s_ref)

    nh = GDN_HEADS
    c = chunk
    sh = int(math.log2(c))
    assert of_ref.shape[1] == 2 * c == LANES
    n_u = 2 * nh
    dirs = ((qf_ref, ktf_ref, gcf_ref, grf_ref), (qb_ref, ktb_ref, gcb_ref, grb_ref))
    cat = lambda xs: jnp.concatenate(xs, axis=0)
    q = cat([r[0][0, 0:nh] for r in dirs])
    k = cat([r[0][0, nh:2 * nh] for r in dirs])
    v = cat([r[0][0, 2 * nh:3 * nh] for r in dirs])
    kt = cat([r[1][0] for r in dirs])
    gcc = cat([jnp.stack([r[2][0, :, d * nh + h:d * nh + h + 1] for h in range(nh)], axis=0)
               for d, r in enumerate(dirs)])
    beta = cat([jnp.stack([r[2][0, :, (2 + d) * nh + h:(2 + d) * nh + h + 1] for h in range(nh)], axis=0)
                for d, r in enumerate(dirs)])
    gcr = cat([jnp.stack([r[3][0, d * nh + h:d * nh + h + 1, :] for h in range(nh)], axis=0)
               for d, r in enumerate(dirs)])

    first = lax.broadcasted_iota(jnp.int32, (n_u, 1, LANES), 2) < c
    rev1 = lax.broadcasted_iota(jnp.int32, (n_u, 1, 1), 0) >= nh
    g_last = (jnp.where(rev1, gcc[:, 0:1], gcc[:, c - 1:c]), jnp.where(rev1, gcc[:, c:c + 1], gcc[:, 2 * c - 1:2 * c]))

    e = jnp.exp(gcc)
    kb = k.astype(F32) * beta
    kbg = (kb * e).astype(BF16)
    vb = (v.astype(F32) * beta).astype(BF16)
    qd = (q.astype(F32) * e).astype(BF16)
    kdt = (kt.astype(F32) * jnp.exp(jnp.where(first, g_last[0], g_last[1]) - gcr)).astype(BF16)

    gram = _bdot(jnp.concatenate([kb.astype(BF16), q], axis=1), kt)
    pack = lambda a, r0: jnp.where(first, a[:, r0:r0 + c], a[:, r0 + c:r0 + 2 * c])
    ui = lax.broadcasted_iota(jnp.int32, (n_u, c, LANES), 0)
    ri = lax.broadcasted_iota(jnp.int32, (n_u, c, LANES), 1)
    lj = jnp.bitwise_and(lax.broadcasted_iota(jnp.int32, (n_u, c, LANES), 2), c - 1)
    incl = jnp.where(ui >= nh, lj - ri, ri - lj) >= 0
    decay = jnp.where(incl, jnp.exp(jnp.where(incl, pack(gcc, 0) - gcr, 0.0)), 0.0)
    m = jnp.where(ri == lj, 0.0, pack(gram, 0) * decay)
    qk = (pack(gram, 2 * c) * decay).astype(BF16)

    rb = lax.broadcasted_iota(jnp.int32, (n_u, 2 * c, LANES), 1)
    lb = lax.broadcasted_iota(jnp.int32, (n_u, 2 * c, LANES), 2)
    same_chunk = jnp.right_shift(rb, sh) == jnp.right_shift(lb, sh)

    def block_diag(a):
        return jnp.where(same_chunk, jnp.concatenate([a, a], axis=1), 0.0)

    mp = -m
    p = jnp.where(ri == lj, 1.0, 0.0) + mp
    mp = _bdot(mp.astype(BF16), block_diag(mp.astype(BF16)))
    for lvl in range(sh - 1):
        mpb = mp.astype(BF16)
        if lvl < sh - 2:
            both = _bdot(jnp.concatenate([p.astype(BF16), mpb], axis=1), block_diag(mpb))
            p = p + both[:, :c]
            mp = both[:, c:]
        else:
            p = p + _bdot(p.astype(BF16), block_diag(mpb))

    uw = _bdot(block_diag(p.astype(BF16)), jnp.concatenate([vb, kbg], axis=2))
    u = uw[:, :, :GDN_D]
    w = uw[:, :, GDN_D:].astype(BF16)

    qk_kdt = jnp.concatenate([qk, kdt], axis=1)
    s = s_ref[...]
    zeros = jnp.zeros((nh, c, GDN_D), BF16)
    for j in range(2):
        chunks = (j, 1 - j)
        rows = [slice(ch * c, (ch + 1) * c) for ch in chunks]
        sel = lambda a: jnp.concatenate([a[:nh, rows[0]], a[nh:, rows[1]]], axis=0)
        ws = _bdot(jnp.concatenate([sel(w), sel(qd)], axis=1), s.astype(BF16))
        v_new = (sel(u) - ws[:, :c]).astype(BF16)
        place = lambda x, ch: jnp.concatenate([x, zeros] if ch == 0 else [zeros, x], axis=1)
        vn = jnp.concatenate([place(v_new[:nh], chunks[0]), place(v_new[nh:], chunks[1])], axis=0)
        ov = _bdot(qk_kdt, vn)
        o = ws[:, c:] + ov[:, :c]
        eg = jnp.exp(jnp.concatenate([g_last[chunks[0]][:nh], g_last[chunks[1]][nh:]], axis=0))
        s = s * eg + ov[:, c:]
        for d, o_ref in enumerate((of_ref, ob_ref)):
            for h in range(nh):
                o_ref[0, rows[d], h * GDN_D:(h + 1) * GDN_D] = o[d * nh + h].astype(o_ref.dtype)
    s_ref[...] = s


def _gdn_scan(qkv, kt, gb, gbt, *, chunk):
    b, n3, s, _ = qkv.shape
    nab = gb.shape[-1]
    r = GDN_BLOCK
    nb = s // r
    mb = META_ROWS // r
    width = GDN_HEADS * GDN_D
    fwd = lambda l: (l + nb - mb) % nb
    bwd = lambda l: (2 * nb - 1 - mb - l) % nb
    qspec = lambda rot: pl.BlockSpec((1, n3, r, GDN_D), lambda bi, l: (bi, 0, rot(l), 0))
    ktspec = lambda rot: pl.BlockSpec((1, GDN_HEADS, GDN_D, r), lambda bi, l: (bi, 0, 0, rot(l)))
    cspec = lambda rot: pl.BlockSpec((1, r, nab), lambda bi, l: (bi, rot(l), 0))
    rspec = lambda rot: pl.BlockSpec((1, nab, r), lambda bi, l: (bi, 0, rot(l)))
    ospec = lambda rot: pl.BlockSpec((1, r, width), lambda bi, l: (bi, rot(l), 0))
    return pl.pallas_call(
        functools.partial(_gdn_scan_kernel, chunk=chunk),
        grid=(b, nb),
        in_specs=[qspec(fwd), qspec(bwd), ktspec(fwd), ktspec(bwd), cspec(fwd), cspec(bwd), rspec(fwd), rspec(bwd)],
        out_specs=[ospec(fwd), ospec(bwd)],
        out_shape=[
            jax.ShapeDtypeStruct((b, s, width), BF16),
            jax.ShapeDtypeStruct((b, s, width), BF16),
        ],
        scratch_shapes=[pltpu.VMEM((2 * GDN_HEADS, GDN_D, GDN_D), F32)],
        compiler_params=pltpu.CompilerParams(dimension_semantics=("parallel", "arbitrary")),
        name="gdn_scan",
    )(qkv, qkv, kt, kt, gb, gb, gbt, gbt)


def _rope(t, cos, sin):
    half = t.shape[-1] // 2
    t1 = t[:, :half]
    t2 = t[:, half:]
    return jnp.concatenate([t1 * cos - t2 * sin, t1 * sin + t2 * cos], axis=-1)


def _ret_scan_kernel(qf_ref, kf_ref, vf_ref, csf_ref, snf_ref, qb_ref, kb_ref, vb_ref, csb_ref, snb_ref,
                     logit_ref, of_ref, ob_ref, r_ref, intra_ref, qpow_ref, kpow_ref):
    c = of_ref.shape[1]
    nh = RET_HEADS
    logit = logit_ref[...]
    log_gamma = jnp.minimum(logit, 0.0) - jnp.log1p(jnp.exp(-jnp.abs(logit)))

    @pl.when(pl.program_id(1) == 0)
    def _():
        r_ref[...] = jnp.zeros_like(r_ref)
        ri = lax.broadcasted_iota(jnp.int32, (c, c), 0)
        ci = lax.broadcasted_iota(jnp.int32, (c, c), 1)
        pos = lax.broadcasted_iota(jnp.int32, (c, 1), 0).astype(F32)
        for d in range(2):
            for hh in range(nh):
                lg = log_gamma[d:d + 1, hh:hh + 1]
                if d == 1:
                    rel = ci - ri
                    q_pow = c - pos
                    k_pow = pos
                else:
                    rel = ri - ci
                    q_pow = pos + 1.0
                    k_pow = c - 1.0 - pos
                incl = rel >= 0
                intra_ref[d * nh + hh] = jnp.where(incl, jnp.exp(jnp.where(incl, rel, 0).astype(F32) * lg), 0.0)
                qpow_ref[d * nh + hh] = jnp.exp(lg * q_pow)
                kpow_ref[d * nh + hh] = jnp.exp(lg * k_pow)

    qrs, krs, vs, cds = [], [], [], []
    dirs = ((qf_ref, kf_ref, vf_ref, csf_ref, snf_ref), (qb_ref, kb_ref, vb_ref, csb_ref, snb_ref))
    for d, (q_ref, k_ref, v_ref, cs_ref, sn_ref) in enumerate(dirs):
        cos = cs_ref[...]
        sin = sn_ref[...]
        for hh in range(nh):
            qs = slice(hh * RET_DK, (hh + 1) * RET_DK)
            qrs.append(_rope(q_ref[0, :, qs].astype(F32), cos, sin))
            krs.append(_rope(k_ref[0, :, qs].astype(F32), cos, sin) * (RET_DK ** -0.5))
            vs.append(v_ref[0, :, hh * RET_DV:(hh + 1) * RET_DV])
            cds.append(jnp.exp(log_gamma[d:d + 1, hh:hh + 1] * float(c)))
    qr = jnp.stack(qrs, axis=0)
    kr = jnp.stack(krs, axis=0)
    v = jnp.stack(vs, axis=0)
    chunk_decay = jnp.stack(cds, axis=0)

    qk = (_bdot_nt(qr.astype(BF16), kr.astype(BF16)) * intra_ref[...]).astype(BF16)
    q_dec = (qr * qpow_ref[...]).astype(BF16)
    k_dec = (kr * kpow_ref[...]).astype(BF16)
    r = r_ref[...]
    o = _bdot(q_dec, r.astype(BF16)) + _bdot(qk, v)
    r_ref[...] = r * chunk_decay + _bdot_tn(k_dec, v)
    for d, o_ref in enumerate((of_ref, ob_ref)):
        for hh in range(nh):
            o_ref[0, :, hh * RET_DV:(hh + 1) * RET_DV] = o[d * nh + hh].astype(o_ref.dtype)


def _ret_scan(proj, cos, sin, logit):
    b, s, _ = proj.shape
    r = RET_CHUNK
    nb = s // r
    mb = META_ROWS // r
    qw = RET_HEADS * RET_DK
    vw = RET_HEADS * RET_DV
    n_units = 2 * RET_HEADS
    q_blk, k_blk, v_blk = 4, 5, 3

    def spec(width, blk, rot):
        return pl.BlockSpec((1, r, width), lambda bi, l: (bi, rot(l), blk))

    fwd = lambda l: (l + nb - mb) % nb
    bwd = lambda l: (2 * nb - 1 - mb - l) % nb
    tab = lambda rot: pl.BlockSpec((r, cos.shape[1]), lambda bi, l: (rot(l), 0))
    out = lambda rot: pl.BlockSpec((1, r, vw), lambda bi, l: (bi, rot(l), 0))
    return pl.pallas_call(
        _ret_scan_kernel,
        grid=(b, nb),
        in_specs=[
            spec(qw, q_blk, fwd), spec(qw, k_blk, fwd), spec(vw, v_blk, fwd), tab(fwd), tab(fwd),
            spec(qw, q_blk, bwd), spec(qw, k_blk, bwd), spec(vw, v_blk, bwd), tab(bwd), tab(bwd),
            pl.BlockSpec(logit.shape, lambda bi, l: (0, 0)),
        ],
        out_specs=[out(fwd), out(bwd)],
        out_shape=[
            jax.ShapeDtypeStruct((b, s, vw), BF16),
            jax.ShapeDtypeStruct((b, s, vw), BF16),
        ],
        scratch_shapes=[
            pltpu.VMEM((n_units, RET_DK, RET_DV), F32),
            pltpu.VMEM((n_units, r, r), F32),
            pltpu.VMEM((n_units, r, 1), F32),
            pltpu.VMEM((n_units, r, 1), F32),
        ],
        compiler_params=pltpu.CompilerParams(dimension_semantics=("parallel", "arbitrary")),
        name="ret_scan",
    )(proj, proj, proj, cos, sin, proj, proj, proj, cos, sin, logit)


def _group_rms(x, width):
    parts = []
    for gidx in range(x.shape[-1] // width):
        xs = x[:, gidx * width:(gidx + 1) * width]
        parts.append(xs * lax.rsqrt(jnp.mean(xs * xs, axis=-1, keepdims=True) + EPS))
    return jnp.concatenate(parts, axis=-1)


def _mix_kernel(of_ref, ob_ref, z_ref, rf_ref, rb_ref, g_ref, ga_ref, gb_ref, h_ref,
                gain_ref, wa_ref, wb_ref, wo_ref, out_ref):
    oa = of_ref[...].astype(F32) + ob_ref[...].astype(F32)
    xa = _group_rms(oa, GDN_D) * gain_ref[...] * _silu(z_ref[...].astype(F32))
    ya = _dot(xa.astype(BF16), wa_ref[...])
    orr = rf_ref[...].astype(F32) + rb_ref[...].astype(F32)
    xb = _group_rms(orr, RET_DV) * _silu(g_ref[...].astype(F32))
    yb = _dot(xb.astype(BF16), wb_ref[...])
    merged = _sigmoid(ga_ref[...].astype(F32)) * ya + _sigmoid(gb_ref[...].astype(F32)) * yb
    out_ref[...] = h_ref[...] + _dot(merged.astype(BF16), wo_ref[...])


def _mix(o_f, o_b, r_f, r_b, proj2, h2, gain_t, wa, wb, wo, *, tm):
    m, d = h2.shape
    va = o_f.shape[1]
    vb = r_f.shape[1]
    row = lambda width, blk: pl.BlockSpec((tm, width), lambda i: (i, blk))
    full = lambda a: pl.BlockSpec(a.shape, lambda i: (0, 0))
    return pl.pallas_call(
        _mix_kernel,
        grid=(m // tm,),
        in_specs=[
            row(va, 0), row(va, 0), row(va, 3), row(vb, 0), row(vb, 0), row(vb, 4), row(d, 10), row(d, 11),
            row(d, 0), full(gain_t), full(wa), full(wb), full(wo),
        ],
        out_specs=row(d, 0),
        out_shape=jax.ShapeDtypeStruct((m, d), F32),
        compiler_params=pltpu.CompilerParams(dimension_semantics=("parallel",)),
        name="mix_out",
    )(o_f, o_b, proj2, r_f, r_b, proj2, proj2, proj2, h2, gain_t, wa, wb, wo)


def _rms_gain(x, gain):
    return x * lax.rsqrt(jnp.mean(x * x, axis=-1, keepdims=True) + EPS) * gain


def _ffn_residual(x, gain, wi_ref, wo_ref):
    gu = _dot(_rms_gain(x, gain).astype(BF16), wi_ref[...])
    hid = gu.shape[1] // 2
    act = (_silu(gu[:, :hid]) * gu[:, hid:]).astype(BF16)
    return x + _dot(act, wo_ref[...])


def _ffn_kernel(h_ref, gain_ref, wi_ref, wo_ref, out_ref):
    out_ref[...] = _ffn_residual(h_ref[...], gain_ref[...], wi_ref, wo_ref)


def _ffn_final_kernel(h_ref, gain_ref, wi_ref, wo_ref, fgain_ref, out_ref):
    out_ref[0] = _rms_gain(_ffn_residual(h_ref[0], gain_ref[...], wi_ref, wo_ref), fgain_ref[...])


def _ffn(h2, gain, wi, wo, *, tm):
    m, d = h2.shape
    full = lambda a: pl.BlockSpec(a.shape, lambda i: (0, 0))
    return pl.pallas_call(
        _ffn_kernel,
        grid=(m // tm,),
        in_specs=[pl.BlockSpec((tm, d), lambda i: (i, 0)), full(gain), full(wi), full(wo)],
        out_specs=pl.BlockSpec((tm, d), lambda i: (i, 0)),
        out_shape=jax.ShapeDtypeStruct((m, d), F32),
        compiler_params=pltpu.CompilerParams(dimension_semantics=("parallel",)),
        name="ffn",
    )(h2, gain, wi, wo)


def _ffn_final(h, gain, wi, wo, final_gain, seq_real, *, tm):
    b, _, d = h.shape
    full = lambda a: pl.BlockSpec(a.shape, lambda bi, i: (0, 0))
    return pl.pallas_call(
        _ffn_final_kernel,
        grid=(b, seq_real // tm),
        in_specs=[pl.BlockSpec((1, tm, d), lambda bi, i: (bi, i, 0)), full(gain), full(wi), full(wo),
                  full(final_gain)],
        out_specs=pl.BlockSpec((1, tm, d), lambda bi, i: (bi, i, 0)),
        out_shape=jax.ShapeDtypeStruct((b, seq_real, d), F32),
        compiler_params=pltpu.CompilerParams(dimension_semantics=("parallel", "parallel")),
        name="ffn_final",
    )(h, gain, wi, wo, final_gain)


def _tile(total, want):
    t = min(want, total)
    while total % t:
        t //= 2
    return t


def kernel(x, meta_tokens, norm_mix, w_in, conv_w, gdn_a_log, gdn_dt_bias, gdn_norm, ret_decay_logit,
           w_up_a, w_up_b, w_out, norm_ffn, w_ffn_in, w_ffn_out, norm_final):
    b, l, d = x.shape
    depth = w_in.shape[0]
    qk_a = GDN_HEADS * GDN_D
    conv_dim = 3 * qk_a
    n_ab = 4 * GDN_HEADS
    assert d == qk_a == RET_HEADS * RET_DK and conv_w.shape[-1] == conv_dim
    assert l % META_ROWS == 0 and META_ROWS % GDN_BLOCK == 0 and GDN_BLOCK % GDN_CHUNK == 0
    assert META_ROWS % RET_CHUNK == 0
    s = l + META_ROWS
    m = b * s

    meta_blk = jnp.concatenate([jnp.zeros((META_ROWS - N_META, d), x.dtype), meta_tokens.astype(x.dtype)], axis=0)
    h = jnp.concatenate([x, jnp.broadcast_to(meta_blk[None], (b, META_ROWS, d))], axis=1)

    half = RET_DK // 2
    row = jnp.arange(s)
    pos = jnp.where(row < l, row + N_META, jnp.maximum(row - l - (META_ROWS - N_META), 0))
    inv = ROPE_BASE ** (-jnp.arange(half, dtype=F32) / half)
    ang = pos.astype(F32)[:, None] * inv[None, :]
    cos, sin = jnp.cos(ang), jnp.sin(ang)

    tm_proj = _tile(m, 1024)
    tm_dense = _tile(m, 256)
    tp = META_ROWS
    h2 = h.reshape(m, d)
    for i in range(depth):
        wi = w_in[i]
        w_main = jnp.concatenate([wi[:, :conv_dim + qk_a], wi[:, conv_dim + qk_a + n_ab:]], axis=1).astype(BF16)
        w_ab = wi[:, conv_dim + qk_a:conv_dim + qk_a + n_ab].astype(BF16)
        proj2, ab2 = _inproj(h2, norm_mix[i][None], w_main, w_ab, tm=tm_proj, tn=w_main.shape[1] // 8)
        proj = proj2.reshape(b, s, -1)

        adt = jnp.stack([jnp.pad(gdn_a_log[i].reshape(-1), (0, n_ab // 2)),
                         jnp.pad(gdn_dt_bias[i].reshape(-1), (0, n_ab // 2))]).astype(F32)
        qkv, kt, gb = _gdn_prep(proj, ab2.reshape(b, s, n_ab), conv_w[i].astype(F32), adt,
                                tp=tp, chunk=GDN_CHUNK, seq_real=l)
        o_f, o_b = _gdn_scan(qkv, kt, gb, jnp.swapaxes(gb, 1, 2), chunk=GDN_CHUNK)
        r_f, r_b = _ret_scan(proj, cos, sin, ret_decay_logit[i].astype(F32))

        gain_t = jnp.tile(gdn_norm[i].astype(F32), GDN_HEADS)[None]
        h2 = _mix(o_f.reshape(m, -1), o_b.reshape(m, -1), r_f.reshape(m, -1), r_b.reshape(m, -1), proj2, h2,
                  gain_t, w_up_a[i].astype(BF16), w_up_b[i].astype(BF16), w_out[i].astype(BF16), tm=tm_dense)
        ffn_w = (norm_ffn[i][None], w_ffn_in[i].astype(BF16), w_ffn_out[i].astype(BF16))
        if i + 1 < depth:
            h2 = _ffn(h2, *ffn_w, tm=tm_dense)
    return _ffn_final(h2.reshape(b, s, d), *ffn_w, norm_final[None], l, tm=_tile(l, 256))
```

```python
import functools
import math

import jax
import jax.numpy as jnp
from jax import lax
from jax.experimental import pallas as pl
from jax.experimental.pallas import tpu as pltpu

N_META = 16
CONV_K = 5
GDN_HEADS = 8
GDN_D = 128
RET_HEADS = 4
RET_DK = 256
RET_DV = 512
ROPE_BASE = 10000.0
EPS = 1e-6

META_ROWS = 256
GDN_BLOCK = 128
GDN_CHUNK = 64
RET_CHUNK = 256
LANES = 128
BF16_SUBLANES = 16

F32 = jnp.float32
BF16 = jnp.bfloat16


def _sigmoid(x):
    return 0.5 * jnp.tanh(0.5 * x) + 0.5


def _silu(x):
    h = 0.5 * x
    return h * jnp.tanh(h) + h


def _dot(a, b):
    return jnp.dot(a, b, preferred_element_type=F32)


def _bdot(a, b):
    return lax.dot_general(a, b, (((2,), (1,)), ((0,), (0,))), preferred_element_type=F32)


def _bdot_nt(a, b):
    return lax.dot_general(a, b, (((2,), (2,)), ((0,), (0,))), preferred_element_type=F32)


def _bdot_tn(a, b):
    return lax.dot_general(a, b, (((1,), (1,)), ((0,), (0,))), preferred_element_type=F32)


def _inproj_kernel(h_ref, gain_ref, w_ref, wab_ref, proj_ref, ab_ref, xn_ref):
    @pl.when(pl.program_id(1) == 0)
    def _():
        x = h_ref[...]
        ms = jnp.mean(x * x, axis=-1, keepdims=True)
        xn = (x * lax.rsqrt(ms + EPS) * gain_ref[...]).astype(BF16)
        xn_ref[...] = xn
        ab_ref[...] = _dot(xn, wab_ref[...])

    proj_ref[...] = _dot(xn_ref[...], w_ref[...]).astype(proj_ref.dtype)


def _inproj(h2, gain, w, wab, *, tm, tn):
    m, d = h2.shape
    n = w.shape[1]
    nab = wab.shape[1]
    return pl.pallas_call(
        _inproj_kernel,
        grid=(m // tm, n // tn),
        in_specs=[
            pl.BlockSpec((tm, d), lambda i, j: (i, 0)),
            pl.BlockSpec((1, d), lambda i, j: (0, 0)),
            pl.BlockSpec((d, tn), lambda i, j: (0, j)),
            pl.BlockSpec((d, nab), lambda i, j: (0, 0)),
        ],
        out_specs=[
            pl.BlockSpec((tm, tn), lambda i, j: (i, j)),
            pl.BlockSpec((tm, nab), lambda i, j: (i, 0)),
        ],
        out_shape=[
            jax.ShapeDtypeStruct((m, n), BF16),
            jax.ShapeDtypeStruct((m, nab), F32),
        ],
        scratch_shapes=[pltpu.VMEM((tm, d), BF16)],
        compiler_params=pltpu.CompilerParams(dimension_semantics=("parallel", "arbitrary")),
        name="inproj",
    )(h2, gain, w, wab)


def _gdn_prep_kernel(main_ref, prev_ref, next_ref, ab_ref, convw_ref, adt_ref, qkv_ref, kt_ref, gb_ref,
                     xe_ref, *, tp, chunk, seq_real):
    i = pl.program_id(1)
    nab = ab_ref.shape[-1]
    half = nab // 2

    ab = ab_ref[0]
    a_log = adt_ref[0:1, :]
    dt = adt_ref[1:2, :]
    xs = ab + dt
    softplus = jnp.maximum(xs, 0.0) + jnp.log1p(jnp.exp(-jnp.abs(xs)))
    g = -jnp.exp(a_log) * softplus
    beta = _sigmoid(ab)
    row = i * tp + lax.broadcasted_iota(jnp.int32, (tp, 1), 0)
    valid = (row < seq_real) | (row >= seq_real + META_ROWS - N_META)
    g = jnp.where(valid, g, 0.0)
    beta = jnp.where(valid, beta, 0.0)

    shift = int(math.log2(chunk))
    ri = lax.broadcasted_iota(jnp.int32, (tp, tp), 0)
    ci = lax.broadcasted_iota(jnp.int32, (tp, tp), 1)
    same = jnp.right_shift(ri, shift) == jnp.right_shift(ci, shift)
    lower = (same & (ci <= ri)).astype(F32)
    upper = (same & (ci >= ri)).astype(F32)
    gc_fwd = jnp.dot(lower, g, precision=lax.Precision.HIGHEST, preferred_element_type=F32)
    gc_bwd = jnp.dot(upper, g, precision=lax.Precision.HIGHEST, preferred_element_type=F32)
    lane = lax.broadcasted_iota(jnp.int32, (tp, nab), 1)
    gc = jnp.where(lane < half // 2, gc_fwd, gc_bwd)
    gb_ref[0] = jnp.where(lane < half, gc, beta)

    pad = (CONV_K - 1) // 2
    halo = 8
    n_groups = main_ref.shape[-1] // LANES
    qk_groups = 2 * GDN_HEADS
    hp = prev_ref.shape[1]
    for gidx in range(n_groups):
        cs = slice(gidx * LANES, (gidx + 1) * LANES)
        xe_ref[0:halo] = prev_ref[0, :, cs].astype(F32)[hp - halo:]
        xe_ref[halo:halo + tp] = main_ref[0, :, cs].astype(F32)
        xe_ref[halo + tp:] = next_ref[0, :, cs].astype(F32)[:halo]
        y = jnp.zeros((tp, LANES), F32)
        for t in range(CONV_K):
            y = y + convw_ref[t:t + 1, cs] * xe_ref[halo + t - pad:halo + t - pad + tp]
        y = _silu(y)
        if gidx < qk_groups:
            y = y * lax.rsqrt(jnp.sum(y * y, axis=-1, keepdims=True) + EPS)
            if gidx < GDN_HEADS:
                y = y * (GDN_D ** -0.5)
        qkv_ref[0, gidx] = y.astype(qkv_ref.dtype)
        if GDN_HEADS <= gidx < qk_groups:
            kt_ref[0, gidx - GDN_HEADS] = y.T.astype(kt_ref.dtype)


def _gdn_prep(proj, ab, conv_w, adt, *, tp, chunk, seq_real):
    b, s, _ = proj.shape
    cdim = conv_w.shape[1]
    nab = ab.shape[-1]
    nhb = s // BF16_SUBLANES
    per = tp // BF16_SUBLANES
    kern = functools.partial(_gdn_prep_kernel, tp=tp, chunk=chunk, seq_real=seq_real)
    return pl.pallas_call(
        kern,
        grid=(b, s // tp),
        in_specs=[
            pl.BlockSpec((1, tp, cdim), lambda bi, i: (bi, i, 0)),
            pl.BlockSpec((1, BF16_SUBLANES, cdim), lambda bi, i: (bi, (i * per + nhb - 1) % nhb, 0)),
            pl.BlockSpec((1, BF16_SUBLANES, cdim), lambda bi, i: (bi, ((i + 1) * per) % nhb, 0)),
            pl.BlockSpec((1, tp, nab), lambda bi, i: (bi, i, 0)),
            pl.BlockSpec((CONV_K, cdim), lambda bi, i: (0, 0)),
            pl.BlockSpec((2, nab), lambda bi, i: (0, 0)),
        ],
        out_specs=[
            pl.BlockSpec((1, 3 * GDN_HEADS, tp, GDN_D), lambda bi, i: (bi, 0, i, 0)),
            pl.BlockSpec((1, GDN_HEADS, GDN_D, tp), lambda bi, i: (bi, 0, 0, i)),
            pl.BlockSpec((1, tp, nab), lambda bi, i: (bi, i, 0)),
        ],
        out_shape=[
            jax.ShapeDtypeStruct((b, 3 * GDN_HEADS, s, GDN_D), BF16),
            jax.ShapeDtypeStruct((b, GDN_HEADS, GDN_D, s), BF16),
            jax.ShapeDtypeStruct((b, s, nab), F32),
        ],
        scratch_shapes=[pltpu.VMEM((tp + 16, LANES), F32)],
        compiler_params=pltpu.CompilerParams(dimension_semantics=("parallel", "parallel")),
        name="gdn_prep",
    )(proj, proj, proj, ab, conv_w, adt)


def _gdn_scan_kernel(qf_ref, qb_ref, ktf_ref, ktb_ref, gcf_ref, gcb_ref, grf_ref, grb_ref, of_ref, ob_ref,
                     s_ref, *, chunk):
    @pl.when(pl.program_id(1) == 0)
    def _():
        s_ref[...] = jnp.zeros_like(s_ref)

    nh = GDN_HEADS
    c = chunk
    sh = int(math.log2(c))
    assert of_ref.shape[1] == 2 * c == LANES
    n_u = 2 * nh
    dirs = ((qf_ref, ktf_ref, gcf_ref, grf_ref), (qb_ref, ktb_ref, gcb_ref, grb_ref))
    cat = lambda xs: jnp.concatenate(xs, axis=0)
    q = cat([r[0][0, 0:nh] for r in dirs])
    k = cat([r[0][0, nh:2 * nh] for r in dirs])
    v = cat([r[0][0, 2 * nh:3 * nh] for r in dirs])
    kt = cat([r[1][0] for r in dirs])
    gcc = cat([jnp.stack([r[2][0, :, d * nh + h:d * nh + h + 1] for h in range(nh)], axis=0)
               for d, r in enumerate(dirs)])
    beta = cat([jnp.stack([r[2][0, :, (2 + d) * nh + h:(2 + d) * nh + h + 1] for h in range(nh)], axis=0)
                for d, r in enumerate(dirs)])
    gcr = cat([jnp.stack([r[3][0, d * nh + h:d * nh + h + 1, :] for h in range(nh)], axis=0)
               for d, r in enumerate(dirs)])

    first = lax.broadcasted_iota(jnp.int32, (n_u, 1, LANES), 2) < c
    rev1 = lax.broadcasted_iota(jnp.int32, (n_u, 1, 1), 0) >= nh
    g_last = (jnp.where(rev1, gcc[:, 0:1], gcc[:, c - 1:c]), jnp.where(rev1, gcc[:, c:c + 1], gcc[:, 2 * c - 1:2 * c]))

    e = jnp.exp(gcc)
    kb = k.astype(F32) * beta
    kbg = (kb * e).astype(BF16)
    vb = (v.astype(F32) * beta).astype(BF16)
    qd = (q.astype(F32) * e).astype(BF16)
    kdt = (kt.astype(F32) * jnp.exp(jnp.where(first, g_last[0], g_last[1]) - gcr)).astype(BF16)

    gram = _bdot(jnp.concatenate([kb.astype(BF16), q], axis=1), kt)
    pack = lambda a, r0: jnp.where(first, a[:, r0:r0 + c], a[:, r0 + c:r0 + 2 * c])
    ui = lax.broadcasted_iota(jnp.int32, (n_u, c, LANES), 0)
    ri = lax.broadcasted_iota(jnp.int32, (n_u, c, LANES), 1)
    lj = jnp.bitwise_and(lax.broadcasted_iota(jnp.int32, (n_u, c, LANES), 2), c - 1)
    incl = jnp.where(ui >= nh, lj - ri, ri - lj) >= 0
    decay = jnp.where(incl, jnp.exp(jnp.where(incl, pack(gcc, 0) - gcr, 0.0)), 0.0)
    m = jnp.where(ri == lj, 0.0, pack(gram, 0) * decay)
    qk = (pack(gram, 2 * c) * decay).astype(BF16)

    rb = lax.broadcasted_iota(jnp.int32, (n_u, 2 * c, LANES), 1)
    lb = lax.broadcasted_iota(jnp.int32, (n_u, 2 * c, LANES), 2)
    same_chunk = jnp.right_shift(rb, sh) == jnp.right_shift(lb, sh)

    def block_diag(a):
        return jnp.where(same_chunk, jnp.concatenate([a, a], axis=1), 0.0)

    mp = -m
    p = jnp.where(ri == lj, 1.0, 0.0) + mp
    mp = _bdot(mp.astype(BF16), block_diag(mp.astype(BF16)))
    for lvl in range(sh - 1):
        mpb = mp.astype(BF16)
        if lvl < sh - 2:
            both = _bdot(jnp.concatenate([p.astype(BF16), mpb], axis=1), block_diag(mpb))
            p = p + both[:, :c]
            mp = both[:, c:]
        else:
            p = p + _bdot(p.astype(BF16), block_diag(mpb))

    uw = _bdot(block_diag(p.astype(BF16)), jnp.concatenate([vb, kbg], axis=2))
    u = uw[:, :, :GDN_D]
    w = uw[:, :, GDN_D:].astype(BF16)

    qk_kdt = jnp.concatenate([qk, kdt], axis=1)
    s = s_ref[...]
    zeros = jnp.zeros((nh, c, GDN_D), BF16)
    for j in range(2):
        chunks = (j, 1 - j)
        rows = [slice(ch * c, (ch + 1) * c) for ch in chunks]
        sel = lambda a: jnp.concatenate([a[:nh, rows[0]], a[nh:, rows[1]]], axis=0)
        ws = _bdot(jnp.concatenate([sel(w), sel(qd)], axis=1), s.astype(BF16))
        v_new = (sel(u) - ws[:, :c]).astype(BF16)
        place = lambda x, ch: jnp.concatenate([x, zeros] if ch == 0 else [zeros, x], axis=1)
        vn = jnp.concatenate([place(v_new[:nh], chunks[0]), place(v_new[nh:], chunks[1])], axis=0)
        ov = _bdot(qk_kdt, vn)
        o = ws[:, c:] + ov[:, :c]
        eg = jnp.exp(jnp.concatenate([g_last[chunks[0]][:nh], g_last[chunks[1]][nh:]], axis=0))
        s = s * eg + ov[:, c:]
        for d, o_ref in enumerate((of_ref, ob_ref)):
            for h in range(nh):
                o_ref[0, rows[d], h * GDN_D:(h + 1) * GDN_D] = o[d * nh + h].astype(o_ref.dtype)
    s_ref[...] = s


def _gdn_scan(qkv, kt, gb, gbt, *, chunk):
    b, n3, s, _ = qkv.shape
    nab = gb.shape[-1]
    r = GDN_BLOCK
    nb = s // r
    mb = META_ROWS // r
    width = GDN_HEADS * GDN_D
    fwd = lambda l: (l + nb - mb) % nb
    bwd = lambda l: (2 * nb - 1 - mb - l) % nb
    qspec = lambda rot: pl.BlockSpec((1, n3, r, GDN_D), lambda bi, l: (bi, 0, rot(l), 0))
    ktspec = lambda rot: pl.BlockSpec((1, GDN_HEADS, GDN_D, r), lambda bi, l: (bi, 0, 0, rot(l)))
    cspec = lambda rot: pl.BlockSpec((1, r, nab), lambda bi, l: (bi, rot(l), 0))
    rspec = lambda rot: pl.BlockSpec((1, nab, r), lambda bi, l: (bi, 0, rot(l)))
    ospec = lambda rot: pl.BlockSpec((1, r, width), lambda bi, l: (bi, rot(l), 0))
    return pl.pallas_call(
        functools.partial(_gdn_scan_kernel, chunk=chunk),
        grid=(b, nb),
        in_specs=[qspec(fwd), qspec(bwd), ktspec(fwd), ktspec(bwd), cspec(fwd), cspec(bwd), rspec(fwd), rspec(bwd)],
        out_specs=[ospec(fwd), ospec(bwd)],
        out_shape=[
            jax.ShapeDtypeStruct((b, s, width), BF16),
            jax.ShapeDtypeStruct((b, s, width), BF16),
        ],
        scratch_shapes=[pltpu.VMEM((2 * GDN_HEADS, GDN_D, GDN_D), F32)],
        compiler_params=pltpu.CompilerParams(dimension_semantics=("parallel", "arbitrary")),
        name="gdn_scan",
    )(qkv, qkv, kt, kt, gb, gb, gbt, gbt)


def _rope(t, cos, sin):
    half = t.shape[-1] // 2
    t1 = t[:, :half]
    t2 = t[:, half:]
    return jnp.concatenate([t1 * cos - t2 * sin, t1 * sin + t2 * cos], axis=-1)


def _ret_scan_kernel(qf_ref, kf_ref, vf_ref, csf_ref, snf_ref, qb_ref, kb_ref, vb_ref, csb_ref, snb_ref,
                     logit_ref, of_ref, ob_ref, r_ref, intra_ref, qpow_ref, kpow_ref):
    c = of_ref.shape[1]
    nh = RET_HEADS
    logit = logit_ref[...]
    log_gamma = jnp.minimum(logit, 0.0) - jnp.log1p(jnp.exp(-jnp.abs(logit)))

    @pl.when(pl.program_id(1) == 0)
    def _():
        r_ref[...] = jnp.zeros_like(r_ref)
        ri = lax.broadcasted_iota(jnp.int32, (c, c), 0)
        ci = lax.broadcasted_iota(jnp.int32, (c, c), 1)
        pos = lax.broadcasted_iota(jnp.int32, (c, 1), 0).astype(F32)
        for d in range(2):
            for hh in range(nh):
                lg = log_gamma[d:d + 1, hh:hh + 1]
                if d == 1:
                    rel = ci - ri
                    q_pow = c - pos
                    k_pow = pos
                else:
                    rel = ri - ci
                    q_pow = pos + 1.0
                    k_pow = c - 1.0 - pos
                incl = rel >= 0
                intra_ref[d * nh + hh] = jnp.where(incl, jnp.exp(jnp.where(incl, rel, 0).astype(F32) * lg), 0.0)
                qpow_ref[d * nh + hh] = jnp.exp(lg * q_pow)
                kpow_ref[d * nh + hh] = jnp.exp(lg * k_pow)

    qrs, krs, vs, cds = [], [], [], []
    dirs = ((qf_ref, kf_ref, vf_ref, csf_ref, snf_ref), (qb_ref, kb_ref, vb_ref, csb_ref, snb_ref))
    for d, (q_ref, k_ref, v_ref, cs_ref, sn_ref) in enumerate(dirs):
        cos = cs_ref[...]
        sin = sn_ref[...]
        for hh in range(nh):
            qs = slice(hh * RET_DK, (hh + 1) * RET_DK)
            qrs.append(_rope(q_ref[0, :, qs].astype(F32), cos, sin))
            krs.append(_rope(k_ref[0, :, qs].astype(F32), cos, sin) * (RET_DK ** -0.5))
            vs.append(v_ref[0, :, hh * RET_DV:(hh + 1) * RET_DV])
            cds.append(jnp.exp(log_gamma[d:d + 1, hh:hh + 1] * float(c)))
    qr = jnp.stack(qrs, axis=0)
    kr = jnp.stack(krs, axis=0)
    v = jnp.stack(vs, axis=0)
    chunk_decay = jnp.stack(cds, axis=0)

    qk = (_bdot_nt(qr.astype(BF16), kr.astype(BF16)) * intra_ref[...]).astype(BF16)
    q_dec = (qr * qpow_ref[...]).astype(BF16)
    k_dec = (kr * kpow_ref[...]).astype(BF16)
    r = r_ref[...]
    o = _bdot(q_dec, r.astype(BF16)) + _bdot(qk, v)
    r_ref[...] = r * chunk_decay + _bdot_tn(k_dec, v)
    for d, o_ref in enumerate((of_ref, ob_ref)):
        for hh in range(nh):
            o_ref[0, :, hh * RET_DV:(hh + 1) * RET_DV] = o[d * nh + hh].astype(o_ref.dtype)


def _ret_scan(proj, cos, sin, logit):
    b, s, _ = proj.shape
    r = RET_CHUNK
    nb = s // r
    mb = META_ROWS // r
    qw = RET_HEADS * RET_DK
    vw = RET_HEADS * RET_DV
    n_units = 2 * RET_HEADS
    q_blk, k_blk, v_blk = 4, 5, 3

    def spec(width, blk, rot):
        return pl.BlockSpec((1, r, width), lambda bi, l: (bi, rot(l), blk))

    fwd = lambda l: (l + nb - mb) % nb
    bwd = lambda l: (2 * nb - 1 - mb - l) % nb
    tab = lambda rot: pl.BlockSpec((r, cos.shape[1]), lambda bi, l: (rot(l), 0))
    out = lambda rot: pl.BlockSpec((1, r, vw), lambda bi, l: (bi, rot(l), 0))
    return pl.pallas_call(
        _ret_scan_kernel,
        grid=(b, nb),
        in_specs=[
            spec(qw, q_blk, fwd), spec(qw, k_blk, fwd), spec(vw, v_blk, fwd), tab(fwd), tab(fwd),
            spec(qw, q_blk, bwd), spec(qw, k_blk, bwd), spec(vw, v_blk, bwd), tab(bwd), tab(bwd),
            pl.BlockSpec(logit.shape, lambda bi, l: (0, 0)),
        ],
        out_specs=[out(fwd), out(bwd)],
        out_shape=[
            jax.ShapeDtypeStruct((b, s, vw), BF16),
            jax.ShapeDtypeStruct((b, s, vw), BF16),
        ],
        scratch_shapes=[
            pltpu.VMEM((n_units, RET_DK, RET_DV), F32),
            pltpu.VMEM((n_units, r, r), F32),
            pltpu.VMEM((n_units, r, 1), F32),
            pltpu.VMEM((n_units, r, 1), F32),
        ],
        compiler_params=pltpu.CompilerParams(dimension_semantics=("parallel", "arbitrary")),
        name="ret_scan",
    )(proj, proj, proj, cos, sin, proj, proj, proj, cos, sin, logit)


def _group_rms(x, width):
    parts = []
    for gidx in range(x.shape[-1] // width):
        xs = x[:, gidx * width:(gidx + 1) * width]
        parts.append(xs * lax.rsqrt(jnp.mean(xs * xs, axis=-1, keepdims=True) + EPS))
    return jnp.concatenate(parts, axis=-1)


def _mix_kernel(of_ref, ob_ref, z_ref, rf_ref, rb_ref, g_ref, ga_ref, gb_ref, h_ref,
                gain_ref, wa_ref, wb_ref, wo_ref, out_ref):
    oa = of_ref[...].astype(F32) + ob_ref[...].astype(F32)
    xa = _group_rms(oa, GDN_D) * gain_ref[...] * _silu(z_ref[...].astype(F32))
    ya = _dot(xa.astype(BF16), wa_ref[...])
    orr = rf_ref[...].astype(F32) + rb_ref[...].astype(F32)
    xb = _group_rms(orr, RET_DV) * _silu(g_ref[...].astype(F32))
    yb = _dot(xb.astype(BF16), wb_ref[...])
    merged = _sigmoid(ga_ref[...].astype(F32)) * ya + _sigmoid(gb_ref[...].astype(F32)) * yb
    out_ref[...] = h_ref[...] + _dot(merged.astype(BF16), wo_ref[...])


def _mix(o_f, o_b, r_f, r_b, proj2, h2, gain_t, wa, wb, wo, *, tm):
    m, d = h2.shape
    va = o_f.shape[1]
    vb = r_f.shape[1]
    row = lambda width, blk: pl.BlockSpec((tm, width), lambda i: (i, blk))
    full = lambda a: pl.BlockSpec(a.shape, lambda i: (0, 0))
    return pl.pallas_call(
        _mix_kernel,
        grid=(m // tm,),
        in_specs=[
            row(va, 0), row(va, 0), row(va, 3), row(vb, 0), row(vb, 0), row(vb, 4), row(d, 10), row(d, 11),
            row(d, 0), full(gain_t), full(wa), full(wb), full(wo),
        ],
        out_specs=row(d, 0),
        out_shape=jax.ShapeDtypeStruct((m, d), F32),
        compiler_params=pltpu.CompilerParams(dimension_semantics=("parallel",)),
        name="mix_out",
    )(o_f, o_b, proj2, r_f, r_b, proj2, proj2, proj2, h2, gain_t, wa, wb, wo)


def _rms_gain(x, gain):
    return x * lax.rsqrt(jnp.mean(x * x, axis=-1, keepdims=True) + EPS) * gain


def _ffn_residual(x, gain, wi_ref, wo_ref):
    gu = _dot(_rms_gain(x, gain).astype(BF16), wi_ref[...])
    hid = gu.shape[1] // 2
    act = (_silu(gu[:, :hid]) * gu[:, hid:]).astype(BF16)
    return x + _dot(act, wo_ref[...])


def _ffn_kernel(h_ref, gain_ref, wi_ref, wo_ref, out_ref):
    out_ref[...] = _ffn_residual(h_ref[...], gain_ref[...], wi_ref, wo_ref)


def _ffn_final_kernel(h_ref, gain_ref, wi_ref, wo_ref, fgain_ref, out_ref):
    out_ref[0] = _rms_gain(_ffn_residual(h_ref[0], gain_ref[...], wi_ref, wo_ref), fgain_ref[...])


def _ffn(h2, gain, wi, wo, *, tm):
    m, d = h2.shape
    full = lambda a: pl.BlockSpec(a.shape, lambda i: (0, 0))
    return pl.pallas_call(
        _ffn_kernel,
        grid=(m // tm,),
        in_specs=[pl.BlockSpec((tm, d), lambda i: (i, 0)), full(gain), full(wi), full(wo)],
        out_specs=pl.BlockSpec((tm, d), lambda i: (i, 0)),
        out_shape=jax.ShapeDtypeStruct((m, d), F32),
        compiler_params=pltpu.CompilerParams(dimension_semantics=("parallel",)),
        name="ffn",
    )(h2, gain, wi, wo)


def _ffn_final(h, gain, wi, wo, final_gain, seq_real, *, tm):
    b, _, d = h.shape
    full = lambda a: pl.BlockSpec(a.shape, lambda bi, i: (0, 0))
    return pl.pallas_call(
        _ffn_final_kernel,
        grid=(b, seq_real // tm),
        in_specs=[pl.BlockSpec((1, tm, d), lambda bi, i: (bi, i, 0)), full(gain), full(wi), full(wo),
                  full(final_gain)],
        out_specs=pl.BlockSpec((1, tm, d), lambda bi, i: (bi, i, 0)),
        out_shape=jax.ShapeDtypeStruct((b, seq_real, d), F32),
        compiler_params=pltpu.CompilerParams(dimension_semantics=("parallel", "parallel")),
        name="ffn_final",
    )(h, gain, wi, wo, final_gain)


def _tile(total, want):
    t = min(want, total)
    while total % t:
        t //= 2
    return t


def kernel(x, meta_tokens, norm_mix, w_in, conv_w, gdn_a_log, gdn_dt_bias, gdn_norm, ret_decay_logit,
           w_up_a, w_up_b, w_out, norm_ffn, w_ffn_in, w_ffn_out, norm_final):
    b, l, d = x.shape
    depth = w_in.shape[0]
    qk_a = GDN_HEADS * GDN_D
    conv_dim = 3 * qk_a
    n_ab = 4 * GDN_HEADS
    assert d == qk_a == RET_HEADS * RET_DK and conv_w.shape[-1] == conv_dim
    assert l % META_ROWS == 0 and META_ROWS % GDN_BLOCK == 0 and GDN_BLOCK % GDN_CHUNK == 0
    assert META_ROWS % RET_CHUNK == 0
    s = l + META_ROWS
    m = b * s

    meta_blk = jnp.concatenate([jnp.zeros((META_ROWS - N_META, d), x.dtype), meta_tokens.astype(x.dtype)], axis=0)
    h = jnp.concatenate([x, jnp.broadcast_to(meta_blk[None], (b, META_ROWS, d))], axis=1)

    half = RET_DK // 2
    row = jnp.arange(s)
    pos = jnp.where(row < l, row + N_META, jnp.maximum(row - l - (META_ROWS - N_META), 0))
    inv = ROPE_BASE ** (-jnp.arange(half, dtype=F32) / half)
    ang = pos.astype(F32)[:, None] * inv[None, :]
    cos, sin = jnp.cos(ang), jnp.sin(ang)

    tm_proj = _tile(m, 2048)
    tm_dense = _tile(m, 256)
    tp = META_ROWS
    h2 = h.reshape(m, d)
    for i in range(depth):
        wi = w_in[i]
        w_main = jnp.concatenate([wi[:, :conv_dim + qk_a], wi[:, conv_dim + qk_a + n_ab:]], axis=1).astype(BF16)
        w_ab = wi[:, conv_dim + qk_a:conv_dim + qk_a + n_ab].astype(BF16)
        proj2, ab2 = _inproj(h2, norm_mix[i][None], w_main, w_ab, tm=tm_proj, tn=w_main.shape[1] // 16)
        proj = proj2.reshape(b, s, -1)

        adt = jnp.stack([jnp.pad(gdn_a_log[i].reshape(-1), (0, n_ab // 2)),
                         jnp.pad(gdn_dt_bias[i].reshape(-1), (0, n_ab // 2))]).astype(F32)
        qkv, kt, gb = _gdn_prep(proj, ab2.reshape(b, s, n_ab), conv_w[i].astype(F32), adt,
                                tp=tp, chunk=GDN_CHUNK, seq_real=l)
        o_f, o_b = _gdn_scan(qkv, kt, gb, jnp.swapaxes(gb, 1, 2), chunk=GDN_CHUNK)
        r_f, r_b = _ret_scan(proj, cos, sin, ret_decay_logit[i].astype(F32))

        gain_t = jnp.tile(gdn_norm[i].astype(F32), GDN_HEADS)[None]
        h2 = _mix(o_f.reshape(m, -1), o_b.reshape(m, -1), r_f.reshape(m, -1), r_b.reshape(m, -1), proj2, h2,
                  gain_t, w_up_a[i].astype(BF16), w_up_b[i].astype(BF16), w_out[i].astype(BF16), tm=tm_dense)
        ffn_w = (norm_ffn[i][None], w_ffn_in[i].astype(BF16), w_ffn_out[i].astype(BF16))
        if i + 1 < depth:
            h2 = _ffn(h2, *ffn_w, tm=tm_dense)
    return _ffn_final(h2.reshape(b, s, d), *ffn_w, norm_final[None], l, tm=_tile(l, 256))
```

```python
import functools
import math

import jax
import jax.numpy as jnp
from jax import lax
from jax.experimental import pallas as pl
from jax.experimental.pallas import tpu as pltpu

N_META = 16
CONV_K = 5
GDN_HEADS = 8
GDN_D = 128
RET_HEADS = 4
RET_DK = 256
RET_DV = 512
ROPE_BASE = 10000.0
EPS = 1e-6

META_ROWS = 256
GDN_BLOCK = 128
GDN_CHUNK = 64
RET_CHUNK = 256
LANES = 128
BF16_SUBLANES = 16

F32 = jnp.float32
BF16 = jnp.bfloat16


def _sigmoid(x):
    return 0.5 * jnp.tanh(0.5 * x) + 0.5


def _silu(x):
    h = 0.5 * x
    return h * jnp.tanh(h) + h


def _dot(a, b):
    return jnp.dot(a, b, preferred_element_type=F32)


def _bdot(a, b):
    return lax.dot_general(a, b, (((2,), (1,)), ((0,), (0,))), preferred_element_type=F32)


def _bdot_nt(a, b):
    return lax.dot_general(a, b, (((2,), (2,)), ((0,), (0,))), preferred_element_type=F32)


def _bdot_tn(a, b):
    return lax.dot_general(a, b, (((1,), (1,)), ((0,), (0,))), preferred_element_type=F32)


def _inproj_kernel(h_ref, gain_ref, w_ref, wab_ref, proj_ref, ab_ref, xn_ref):
    @pl.when(pl.program_id(1) == 0)
    def _():
        x = h_ref[...]
        ms = jnp.mean(x * x, axis=-1, keepdims=True)
        xn = (x * lax.rsqrt(ms + EPS) * gain_ref[...]).astype(BF16)
        xn_ref[...] = xn
        ab_ref[...] = _dot(xn, wab_ref[...])

    proj_ref[...] = _dot(xn_ref[...], w_ref[...]).astype(proj_ref.dtype)


def _inproj(h2, gain, w, wab, *, tm, tn):
    m, d = h2.shape
    n = w.shape[1]
    nab = wab.shape[1]
    return pl.pallas_call(
        _inproj_kernel,
        grid=(m // tm, n // tn),
        in_specs=[
            pl.BlockSpec((tm, d), lambda i, j: (i, 0)),
            pl.BlockSpec((1, d), lambda i, j: (0, 0)),
            pl.BlockSpec((d, tn), lambda i, j: (0, j)),
            pl.BlockSpec((d, nab), lambda i, j: (0, 0)),
        ],
        out_specs=[
            pl.BlockSpec((tm, tn), lambda i, j: (i, j)),
            pl.BlockSpec((tm, nab), lambda i, j: (i, 0)),
        ],
        out_shape=[
            jax.ShapeDtypeStruct((m, n), BF16),
            jax.ShapeDtypeStruct((m, nab), F32),
        ],
        scratch_shapes=[pltpu.VMEM((tm, d), BF16)],
        compiler_params=pltpu.CompilerParams(dimension_semantics=("parallel", "arbitrary")),
        name="inproj",
    )(h2, gain, w, wab)


def _gdn_prep_kernel(main_ref, prev_ref, next_ref, ab_ref, convw_ref, adt_ref, qkv_ref, kt_ref, gb_ref,
                     xe_ref, *, tp, chunk, seq_real):
    i = pl.program_id(1)
    nab = ab_ref.shape[-1]
    half = nab // 2

    ab = ab_ref[0]
    a_log = adt_ref[0:1, :]
    dt = adt_ref[1:2, :]
    xs = ab + dt
    softplus = jnp.maximum(xs, 0.0) + jnp.log1p(jnp.exp(-jnp.abs(xs)))
    g = -jnp.exp(a_log) * softplus
    beta = _sigmoid(ab)
    row = i * tp + lax.broadcasted_iota(jnp.int32, (tp, 1), 0)
    valid = (row < seq_real) | (row >= seq_real + META_ROWS - N_META)
    g = jnp.where(valid, g, 0.0)
    beta = jnp.where(valid, beta, 0.0)

    shift = int(math.log2(chunk))
    ri = lax.broadcasted_iota(jnp.int32, (tp, tp), 0)
    ci = lax.broadcasted_iota(jnp.int32, (tp, tp), 1)
    same = jnp.right_shift(ri, shift) == jnp.right_shift(ci, shift)
    lower = (same & (ci <= ri)).astype(F32)
    upper = (same & (ci >= ri)).astype(F32)
    gc_fwd = jnp.dot(lower, g, precision=lax.Precision.HIGHEST, preferred_element_type=F32)
    gc_bwd = jnp.dot(upper, g, precision=lax.Precision.HIGHEST, preferred_element_type=F32)
    lane = lax.broadcasted_iota(jnp.int32, (tp, nab), 1)
    gc = jnp.where(lane < half // 2, gc_fwd, gc_bwd)
    gb_ref[0] = jnp.where(lane < half, gc, beta)

    pad = (CONV_K - 1) // 2
    halo = 8
    n_groups = main_ref.shape[-1] // LANES
    qk_groups = 2 * GDN_HEADS
    hp = prev_ref.shape[1]
    for gidx in range(n_groups):
        cs = slice(gidx * LANES, (gidx + 1) * LANES)
        xe_ref[0:halo] = prev_ref[0, :, cs].astype(F32)[hp - halo:]
        xe_ref[halo:halo + tp] = main_ref[0, :, cs].astype(F32)
        xe_ref[halo + tp:] = next_ref[0, :, cs].astype(F32)[:halo]
        y = jnp.zeros((tp, LANES), F32)
        for t in range(CONV_K):
            y = y + convw_ref[t:t + 1, cs] * xe_ref[halo + t - pad:halo + t - pad + tp]
        y = _silu(y)
        if gidx < qk_groups:
            y = y * lax.rsqrt(jnp.sum(y * y, axis=-1, keepdims=True) + EPS)
            if gidx < GDN_HEADS:
                y = y * (GDN_D ** -0.5)
        qkv_ref[0, gidx] = y.astype(qkv_ref.dtype)
        if GDN_HEADS <= gidx < qk_groups:
            kt_ref[0, gidx - GDN_HEADS] = y.T.astype(kt_ref.dtype)


def _gdn_prep(proj, ab, conv_w, adt, *, tp, chunk, seq_real):
    b, s, _ = proj.shape
    cdim = conv_w.shape[1]
    nab = ab.shape[-1]
    nhb = s // BF16_SUBLANES
    per = tp // BF16_SUBLANES
    kern = functools.partial(_gdn_prep_kernel, tp=tp, chunk=chunk, seq_real=seq_real)
    return pl.pallas_call(
        kern,
        grid=(b, s // tp),
        in_specs=[
            pl.BlockSpec((1, tp, cdim), lambda bi, i: (bi, i, 0)),
            pl.BlockSpec((1, BF16_SUBLANES, cdim), lambda bi, i: (bi, (i * per + nhb - 1) % nhb, 0)),
            pl.BlockSpec((1, BF16_SUBLANES, cdim), lambda bi, i: (bi, ((i + 1) * per) % nhb, 0)),
            pl.BlockSpec((1, tp, nab), lambda bi, i: (bi, i, 0)),
            pl.BlockSpec((CONV_K, cdim), lambda bi, i: (0, 0)),
            pl.BlockSpec((2, nab), lambda bi, i: (0, 0)),
        ],
        out_specs=[
            pl.BlockSpec((1, 3 * GDN_HEADS, tp, GDN_D), lambda bi, i: (bi, 0, i, 0)),
            pl.BlockSpec((1, GDN_HEADS, GDN_D, tp), lambda bi, i: (bi, 0, 0, i)),
            pl.BlockSpec((1, tp, nab), lambda bi, i: (bi, i, 0)),
        ],
        out_shape=[
            jax.ShapeDtypeStruct((b, 3 * GDN_HEADS, s, GDN_D), BF16),
            jax.ShapeDtypeStruct((b, GDN_HEADS, GDN_D, s), BF16),
            jax.ShapeDtypeStruct((b, s, nab), F32),
        ],
        scratch_shapes=[pltpu.VMEM((tp + 16, LANES), F32)],
        compiler_params=pltpu.CompilerParams(dimension_semantics=("parallel", "parallel")),
        name="gdn_prep",
    )(proj, proj, proj, ab, conv_w, adt)


def _gdn_scan_kernel(qf_ref, qb_ref, ktf_ref, ktb_ref, gcf_ref, gcb_ref, grf_ref, grb_ref, of_ref, ob_ref,
                     s_ref, *, chunk):
    @pl.when(pl.program_id(1) == 0)
    def _():
        s_ref[...] = jnp.zeros_like(s_ref)

    nh = GDN_HEADS
    c = chunk
    sh = int(math.log2(c))
    assert of_ref.shape[1] == 2 * c == LANES
    n_u = 2 * nh
    dirs = ((qf_ref, ktf_ref, gcf_ref, grf_ref), (qb_ref, ktb_ref, gcb_ref, grb_ref))
    cat = lambda xs: jnp.concatenate(xs, axis=0)
    q = cat([r[0][0, 0:nh] for r in dirs])
    k = cat([r[0][0, nh:2 * nh] for r in dirs])
    v = cat([r[0][0, 2 * nh:3 * nh] for r in dirs])
    kt = cat([r[1][0] for r in dirs])
    gcc = cat([jnp.stack([r[2][0, :, d * nh + h:d * nh + h + 1] for h in range(nh)], axis=0)
               for d, r in enumerate(dirs)])
    beta = cat([jnp.stack([r[2][0, :, (2 + d) * nh + h:(2 + d) * nh + h + 1] for h in range(nh)], axis=0)
                for d, r in enumerate(dirs)])
    gcr = cat([jnp.stack([r[3][0, d * nh + h:d * nh + h + 1, :] for h in range(nh)], axis=0)
               for d, r in enumerate(dirs)])

    first = lax.broadcasted_iota(jnp.int32, (n_u, 1, LANES), 2) < c
    rev1 = lax.broadcasted_iota(jnp.int32, (n_u, 1, 1), 0) >= nh
    g_last = (jnp.where(rev1, gcc[:, 0:1], gcc[:, c - 1:c]), jnp.where(rev1, gcc[:, c:c + 1], gcc[:, 2 * c - 1:2 * c]))

    e = jnp.exp(gcc)
    kb = k.astype(F32) * beta
    kbg = (kb * e).astype(BF16)
    vb = (v.astype(F32) * beta).astype(BF16)
    qd = (q.astype(F32) * e).astype(BF16)
    kdt = (kt.astype(F32) * jnp.exp(jnp.where(first, g_last[0], g_last[1]) - gcr)).astype(BF16)

    gram = _bdot(jnp.concatenate([kb.astype(BF16), q], axis=1), kt)
    pack = lambda a, r0: jnp.where(first, a[:, r0:r0 + c], a[:, r0 + c:r0 + 2 * c])
    ui = lax.broadcasted_iota(jnp.int32, (n_u, c, LANES), 0)
    ri = lax.broadcasted_iota(jnp.int32, (n_u, c, LANES), 1)
    lj = jnp.bitwise_and(lax.broadcasted_iota(jnp.int32, (n_u, c, LANES), 2), c - 1)
    incl = jnp.where(ui >= nh, lj - ri, ri - lj) >= 0
    decay = jnp.where(incl, jnp.exp(jnp.where(incl, pack(gcc, 0) - gcr, 0.0)), 0.0)
    m = jnp.where(ri == lj, 0.0, pack(gram, 0) * decay)
    qk = (pack(gram, 2 * c) * decay).astype(BF16)

    rb = lax.broadcasted_iota(jnp.int32, (n_u, 2 * c, LANES), 1)
    lb = lax.broadcasted_iota(jnp.int32, (n_u, 2 * c, LANES), 2)
    same_chunk = jnp.right_shift(rb, sh) == jnp.right_shift(lb, sh)

    def block_diag(a):
        return jnp.where(same_chunk, jnp.concatenate([a, a], axis=1), 0.0)

    mp = -m
    p = jnp.where(ri == lj, 1.0, 0.0) + mp
    mp = _bdot(mp.astype(BF16), block_diag(mp.astype(BF16)))
    for lvl in range(sh - 1):
        mpb = mp.astype(BF16)
        if lvl < sh - 2:
            both = _bdot(jnp.concatenate([p.astype(BF16), mpb], axis=1), block_diag(mpb))
            p = p + both[:, :c]
            mp = both[:, c:]
        else:
            p = p + _bdot(p.astype(BF16), block_diag(mpb))

    uw = _bdot(block_diag(p.astype(BF16)), jnp.concatenate([vb, kbg], axis=2))
    u = uw[:, :, :GDN_D]
    w = uw[:, :, GDN_D:].astype(BF16)

    qk_kdt = jnp.concatenate([qk, kdt], axis=1)
    s = s_ref[...]
    zeros = jnp.zeros((nh, c, GDN_D), BF16)
    for j in range(2):
        chunks = (j, 1 - j)
        rows = [slice(ch * c, (ch + 1) * c) for ch in chunks]
        sel = lambda a: jnp.concatenate([a[:nh, rows[0]], a[nh:, rows[1]]], axis=0)
        ws = _bdot(jnp.concatenate([sel(w), sel(qd)], axis=1), s.astype(BF16))
        v_new = (sel(u) - ws[:, :c]).astype(BF16)
        place = lambda x, ch: jnp.concatenate([x, zeros] if ch == 0 else [zeros, x], axis=1)
        vn = jnp.concatenate([place(v_new[:nh], chunks[0]), place(v_new[nh:], chunks[1])], axis=0)
        ov = _bdot(qk_kdt, vn)
        o = ws[:, c:] + ov[:, :c]
        eg = jnp.exp(jnp.concatenate([g_last[chunks[0]][:nh], g_last[chunks[1]][nh:]], axis=0))
        s = s * eg + ov[:, c:]
        for d, o_ref in enumerate((of_ref, ob_ref)):
            for h in range(nh):
                o_ref[0, rows[d], h * GDN_D:(h + 1) * GDN_D] = o[d * nh + h].astype(o_ref.dtype)
    s_ref[...] = s


def _gdn_scan(qkv, kt, gb, gbt, *, chunk):
    b, n3, s, _ = qkv.shape
    nab = gb.shape[-1]
    r = GDN_BLOCK
    nb = s // r
    mb = META_ROWS // r
    width = GDN_HEADS * GDN_D
    fwd = lambda l: (l + nb - mb) % nb
    bwd = lambda l: (2 * nb - 1 - mb - l) % nb
    qspec = lambda rot: pl.BlockSpec((1, n3, r, GDN_D), lambda bi, l: (bi, 0, rot(l), 0))
    ktspec = lambda rot: pl.BlockSpec((1, GDN_HEADS, GDN_D, r), lambda bi, l: (bi, 0, 0, rot(l)))
    cspec = lambda rot: pl.BlockSpec((1, r, nab), lambda bi, l: (bi, rot(l), 0))
    rspec = lambda rot: pl.BlockSpec((1, nab, r), lambda bi, l: (bi, 0, rot(l)))
    ospec = lambda rot: pl.BlockSpec((1, r, width), lambda bi, l: (bi, rot(l), 0))
    return pl.pallas_call(
        functools.partial(_gdn_scan_kernel, chunk=chunk),
        grid=(b, nb),
        in_specs=[qspec(fwd), qspec(bwd), ktspec(fwd), ktspec(bwd), cspec(fwd), cspec(bwd), rspec(fwd), rspec(bwd)],
        out_specs=[ospec(fwd), ospec(bwd)],
        out_shape=[
            jax.ShapeDtypeStruct((b, s, width), BF16),
            jax.ShapeDtypeStruct((b, s, width), BF16),
        ],
        scratch_shapes=[pltpu.VMEM((2 * GDN_HEADS, GDN_D, GDN_D), F32)],
        compiler_params=pltpu.CompilerParams(dimension_semantics=("parallel", "arbitrary")),
        name="gdn_scan",
    )(qkv, qkv, kt, kt, gb, gb, gbt, gbt)


def _rope(t, cos, sin):
    half = t.shape[-1] // 2
    t1 = t[:, :half]
    t2 = t[:, half:]
    return jnp.concatenate([t1 * cos - t2 * sin, t1 * sin + t2 * cos], axis=-1)


def _ret_scan_kernel(qf_ref, kf_ref, vf_ref, csf_ref, snf_ref, qb_ref, kb_ref, vb_ref, csb_ref, snb_ref,
                     logit_ref, of_ref, ob_ref, r_ref, intra_ref, qpow_ref, kpow_ref):
    c = of_ref.shape[1]
    nh = RET_HEADS
    logit = logit_ref[...]
    log_gamma = jnp.minimum(logit, 0.0) - jnp.log1p(jnp.exp(-jnp.abs(logit)))

    @pl.when(pl.program_id(1) == 0)
    def _():
        r_ref[...] = jnp.zeros_like(r_ref)
        ri = lax.broadcasted_iota(jnp.int32, (c, c), 0)
        ci = lax.broadcasted_iota(jnp.int32, (c, c), 1)
        pos = lax.broadcasted_iota(jnp.int32, (c, 1), 0).astype(F32)
        for d in range(2):
            for hh in range(nh):
                lg = log_gamma[d:d + 1, hh:hh + 1]
                if d == 1:
                    rel = ci - ri
                    q_pow = c - pos
                    k_pow = pos
                else:
                    rel = ri - ci
                    q_pow = pos + 1.0
                    k_pow = c - 1.0 - pos
                incl = rel >= 0
                intra_ref[d * nh + hh] = jnp.where(incl, jnp.exp(jnp.where(incl, rel, 0).astype(F32) * lg), 0.0)
                qpow_ref[d * nh + hh] = jnp.exp(lg * q_pow)
                kpow_ref[d * nh + hh] = jnp.exp(lg * k_pow)

    qrs, krs, vs, cds = [], [], [], []
    dirs = ((qf_ref, kf_ref, vf_ref, csf_ref, snf_ref), (qb_ref, kb_ref, vb_ref, csb_ref, snb_ref))
    for d, (q_ref, k_ref, v_ref, cs_ref, sn_ref) in enumerate(dirs):
        cos = cs_ref[...]
        sin = sn_ref[...]
        for hh in range(nh):
            qs = slice(hh * RET_DK, (hh + 1) * RET_DK)
            qrs.append(_rope(q_ref[0, :, qs].astype(F32), cos, sin))
            krs.append(_rope(k_ref[0, :, qs].astype(F32), cos, sin) * (RET_DK ** -0.5))
            vs.append(v_ref[0, :, hh * RET_DV:(hh + 1) * RET_DV])
            cds.append(jnp.exp(log_gamma[d:d + 1, hh:hh + 1] * float(c)))
    qr = jnp.stack(qrs, axis=0)
    kr = jnp.stack(krs, axis=0)
    v = jnp.stack(vs, axis=0)
    chunk_decay = jnp.stack(cds, axis=0)

    qk = (_bdot_nt(qr.astype(BF16), kr.astype(BF16)) * intra_ref[...]).astype(BF16)
    q_dec = (qr * qpow_ref[...]).astype(BF16)
    k_dec = (kr * kpow_ref[...]).astype(BF16)
    r = r_ref[...]
    o = _bdot(q_dec, r.astype(BF16)) + _bdot(qk, v)
    r_ref[...] = r * chunk_decay + _bdot_tn(k_dec, v)
    for d, o_ref in enumerate((of_ref, ob_ref)):
        for hh in range(nh):
            o_ref[0, :, hh * RET_DV:(hh + 1) * RET_DV] = o[d * nh + hh].astype(o_ref.dtype)


def _ret_scan(proj, cos, sin, logit):
    b, s, _ = proj.shape
    r = RET_CHUNK
    nb = s // r
    mb = META_ROWS // r
    qw = RET_HEADS * RET_DK
    vw = RET_HEADS * RET_DV
    n_units = 2 * RET_HEADS
    q_blk, k_blk, v_blk = 4, 5, 3

    def spec(width, blk, rot):
        return pl.BlockSpec((1, r, width), lambda bi, l: (bi, rot(l), blk))

    fwd = lambda l: (l + nb - mb) % nb
    bwd = lambda l: (2 * nb - 1 - mb - l) % nb
    tab = lambda rot: pl.BlockSpec((r, cos.shape[1]), lambda bi, l: (rot(l), 0))
    out = lambda rot: pl.BlockSpec((1, r, vw), lambda bi, l: (bi, rot(l), 0))
    return pl.pallas_call(
        _ret_scan_kernel,
        grid=(b, nb),
        in_specs=[
            spec(qw, q_blk, fwd), spec(qw, k_blk, fwd), spec(vw, v_blk, fwd), tab(fwd), tab(fwd),
            spec(qw, q_blk, bwd), spec(qw, k_blk, bwd), spec(vw, v_blk, bwd), tab(bwd), tab(bwd),
            pl.BlockSpec(logit.shape, lambda bi, l: (0, 0)),
        ],
        out_specs=[out(fwd), out(bwd)],
        out_shape=[
            jax.ShapeDtypeStruct((b, s, vw), BF16),
            jax.ShapeDtypeStruct((b, s, vw), BF16),
        ],
        scratch_shapes=[
            pltpu.VMEM((n_units, RET_DK, RET_DV), F32),
            pltpu.VMEM((n_units, r, r), F32),
            pltpu.VMEM((n_units, r, 1), F32),
            pltpu.VMEM((n_units, r, 1), F32),
        ],
        compiler_params=pltpu.CompilerParams(dimension_semantics=("parallel", "arbitrary")),
        name="ret_scan",
    )(proj, proj, proj, cos, sin, proj, proj, proj, cos, sin, logit)


def _group_rms(x, width):
    parts = []
    for gidx in range(x.shape[-1] // width):
        xs = x[:, gidx * width:(gidx + 1) * width]
        parts.append(xs * lax.rsqrt(jnp.mean(xs * xs, axis=-1, keepdims=True) + EPS))
    return jnp.concatenate(parts, axis=-1)


def _mix_kernel(of_ref, ob_ref, z_ref, rf_ref, rb_ref, g_ref, ga_ref, gb_ref, h_ref,
                gain_ref, wa_ref, wb_ref, wo_ref, out_ref):
    oa = of_ref[...].astype(F32) + ob_ref[...].astype(F32)
    xa = _group_rms(oa, GDN_D) * gain_ref[...] * _silu(z_ref[...]).astype(F32)
    ya = _dot(xa.astype(BF16), wa_ref[...])
    orr = rf_ref[...].astype(F32) + rb_ref[...].astype(F32)
    xb = _group_rms(orr, RET_DV) * _silu(g_ref[...]).astype(F32)
    yb = _dot(xb.astype(BF16), wb_ref[...])
    merged = _sigmoid(ga_ref[...]).astype(F32) * ya + _sigmoid(gb_ref[...]).astype(F32) * yb
    out_ref[...] = h_ref[...] + _dot(merged.astype(BF16), wo_ref[...])


def _mix(o_f, o_b, r_f, r_b, proj2, h2, gain_t, wa, wb, wo, *, tm):
    m, d = h2.shape
    va = o_f.shape[1]
    vb = r_f.shape[1]
    row = lambda width, blk: pl.BlockSpec((tm, width), lambda i: (i, blk))
    full = lambda a: pl.BlockSpec(a.shape, lambda i: (0, 0))
    return pl.pallas_call(
        _mix_kernel,
        grid=(m // tm,),
        in_specs=[
            row(va, 0), row(va, 0), row(va, 3), row(vb, 0), row(vb, 0), row(vb, 4), row(d, 10), row(d, 11),
            row(d, 0), full(gain_t), full(wa), full(wb), full(wo),
        ],
        out_specs=row(d, 0),
        out_shape=jax.ShapeDtypeStruct((m, d), F32),
        compiler_params=pltpu.CompilerParams(dimension_semantics=("parallel",)),
        name="mix_out",
    )(o_f, o_b, proj2, r_f, r_b, proj2, proj2, proj2, h2, gain_t, wa, wb, wo)


def _rms_gain(x, gain):
    return x * lax.rsqrt(jnp.mean(x * x, axis=-1, keepdims=True) + EPS) * gain


def _ffn_residual(x, gain, wi_ref, wo_ref):
    gu = _dot(_rms_gain(x, gain).astype(BF16), wi_ref[...])
    hid = gu.shape[1] // 2
    act = (_silu(gu[:, :hid]) * gu[:, hid:]).astype(BF16)
    return x + _dot(act, wo_ref[...])


def _ffn_kernel(h_ref, gain_ref, wi_ref, wo_ref, out_ref):
    out_ref[...] = _ffn_residual(h_ref[...], gain_ref[...], wi_ref, wo_ref)


def _ffn_final_kernel(h_ref, gain_ref, wi_ref, wo_ref, fgain_ref, out_ref):
    out_ref[0] = _rms_gain(_ffn_residual(h_ref[0], gain_ref[...], wi_ref, wo_ref), fgain_ref[...])


def _ffn(h2, gain, wi, wo, *, tm):
    m, d = h2.shape
    full = lambda a: pl.BlockSpec(a.shape, lambda i: (0, 0))
    return pl.pallas_call(
        _ffn_kernel,
        grid=(m // tm,),
        in_specs=[pl.BlockSpec((tm, d), lambda i: (i, 0)), full(gain), full(wi), full(wo)],
        out_specs=pl.BlockSpec((tm, d), lambda i: (i, 0)),
        out_shape=jax.ShapeDtypeStruct((m, d), F32),
        compiler_params=pltpu.CompilerParams(dimension_semantics=("parallel",)),
        name="ffn",
    )(h2, gain, wi, wo)


def _ffn_final(h, gain, wi, wo, final_gain, seq_real, *, tm):
    b, _, d = h.shape
    full = lambda a: pl.BlockSpec(a.shape, lambda bi, i: (0, 0))
    return pl.pallas_call(
        _ffn_final_kernel,
        grid=(b, seq_real // tm),
        in_specs=[pl.BlockSpec((1, tm, d), lambda bi, i: (bi, i, 0)), full(gain), full(wi), full(wo),
                  full(final_gain)],
        out_specs=pl.BlockSpec((1, tm, d), lambda bi, i: (bi, i, 0)),
        out_shape=jax.ShapeDtypeStruct((b, seq_real, d), F32),
        compiler_params=pltpu.CompilerParams(dimension_semantics=("parallel", "parallel")),
        name="ffn_final",
    )(h, gain, wi, wo, final_gain)


def _tile(total, want):
    t = min(want, total)
    while total % t:
        t //= 2
    return t


def kernel(x, meta_tokens, norm_mix, w_in, conv_w, gdn_a_log, gdn_dt_bias, gdn_norm, ret_decay_logit,
           w_up_a, w_up_b, w_out, norm_ffn, w_ffn_in, w_ffn_out, norm_final):
    b, l, d = x.shape
    depth = w_in.shape[0]
    qk_a = GDN_HEADS * GDN_D
    conv_dim = 3 * qk_a
    n_ab = 4 * GDN_HEADS
    assert d == qk_a == RET_HEADS * RET_DK and conv_w.shape[-1] == conv_dim
    assert l % META_ROWS == 0 and META_ROWS % GDN_BLOCK == 0 and GDN_BLOCK % GDN_CHUNK == 0
    assert META_ROWS % RET_CHUNK == 0
    s = l + META_ROWS
    m = b * s

    meta_blk = jnp.concatenate([jnp.zeros((META_ROWS - N_META, d), x.dtype), meta_tokens.astype(x.dtype)], axis=0)
    h = jnp.concatenate([x, jnp.broadcast_to(meta_blk[None], (b, META_ROWS, d))], axis=1)

    half = RET_DK // 2
    row = jnp.arange(s)
    pos = jnp.where(row < l, row + N_META, jnp.maximum(row - l - (META_ROWS - N_META), 0))
    inv = ROPE_BASE ** (-jnp.arange(half, dtype=F32) / half)
    ang = pos.astype(F32)[:, None] * inv[None, :]
    cos, sin = jnp.cos(ang), jnp.sin(ang)

    tm_proj = _tile(m, 1024)
    tm_dense = _tile(m, 256)
    tp = META_ROWS
    h2 = h.reshape(m, d)
    for i in range(depth):
        wi = w_in[i]
        w_main = jnp.concatenate([wi[:, :conv_dim + qk_a], wi[:, conv_dim + qk_a + n_ab:]], axis=1).astype(BF16)
        w_ab = wi[:, conv_dim + qk_a:conv_dim + qk_a + n_ab].astype(BF16)
        proj2, ab2 = _inproj(h2, norm_mix[i][None], w_main, w_ab, tm=tm_proj, tn=w_main.shape[1] // 8)
        proj = proj2.reshape(b, s, -1)

        adt = jnp.stack([jnp.pad(gdn_a_log[i].reshape(-1), (0, n_ab // 2)),
                         jnp.pad(gdn_dt_bias[i].reshape(-1), (0, n_ab // 2))]).astype(F32)
        qkv, kt, gb = _gdn_prep(proj, ab2.reshape(b, s, n_ab), conv_w[i].astype(F32), adt,
                                tp=tp, chunk=GDN_CHUNK, seq_real=l)
        o_f, o_b = _gdn_scan(qkv, kt, gb, jnp.swapaxes(gb, 1, 2), chunk=GDN_CHUNK)
        r_f, r_b = _ret_scan(proj, cos, sin, ret_decay_logit[i].astype(F32))

        gain_t = jnp.tile(gdn_norm[i].astype(F32), GDN_HEADS)[None]
        h2 = _mix(o_f.reshape(m, -1), o_b.reshape(m, -1), r_f.reshape(m, -1), r_b.reshape(m, -1), proj2, h2,
                  gain_t, w_up_a[i].astype(BF16), w_up_b[i].astype(BF16), w_out[i].astype(BF16), tm=tm_dense)
        ffn_w = (norm_ffn[i][None], w_ffn_in[i].astype(BF16), w_ffn_out[i].astype(BF16))
        if i + 1 < depth:
            h2 = _ffn(h2, *ffn_w, tm=tm_dense)
    return _ffn_final(h2.reshape(b, s, d), *ffn_w, norm_final[None], l, tm=_tile(l, 256))
```
